```python
import math
import jax, jax.numpy as jnp
from jax import lax
import numpy as np

D_MODEL = 1024
BATCH = 32
SEQ = 256
DEPTH = 4
DEC_BATCH = 2
DEC_SEQ = 1024
PAST_LEN = 512

GRID_W = 64
N_MIXERS = 2
N_SSD = (DEPTH + 1) // 2
N_FNO = DEPTH // 2
EXPAND = 2
D_INNER = EXPAND * D_MODEL
HEAD_DIM = 64
N_HEADS = D_INNER // HEAD_DIM
N_GROUPS = 4
D_STATE = 128
CONV_W = 3
CHUNK = 128
D_XBC = D_INNER + 2 * N_GROUPS * D_STATE
D_IN_PROJ = D_INNER + D_XBC + 2 * N_HEADS
N_FGROUPS = 4
D_FF = 4 * D_MODEL
EPS = 1e-6

kernel_name = "bidir_ssd_fnet_hybrid_diffusion_step"


def rmsnorm(x, w):
    xf = x.astype(jnp.float32)
    xf = xf * lax.rsqrt(jnp.mean(xf * xf, axis=-1, keepdims=True) + EPS)
    return xf.astype(x.dtype) * w


def ada_modulation(cond, w, b):
    m = jax.nn.silu(cond) @ w + b
    return jnp.split(m[:, None, :], 6, axis=-1)


def centred_dwconv(x, w, bias, n_seg):
    b, L, C = x.shape
    seg = L // n_seg
    xs = x.reshape(b * n_seg, seg, C)
    pad = CONV_W // 2
    xp = jnp.pad(xs, ((0, 0), (pad, pad), (0, 0)))
    out = sum(xp[:, k:k + seg] * w[k] for k in range(CONV_W)) + bias
    return out.reshape(b, L, C)


def ssd_scan(x, dt, A, Bm, Cm, h0):
    b, L, H, P = x.shape
    G, N = Bm.shape[2], Bm.shape[3]
    R = H // G
    nc = L // CHUNK
    f32 = jnp.float32
    xc = x.astype(f32).reshape(b, nc, CHUNK, G, R, P)
    dtc = dt.astype(f32).reshape(b, nc, CHUNK, G, R)
    Bc = Bm.astype(f32).reshape(b, nc, CHUNK, G, N)
    Cc = Cm.astype(f32).reshape(b, nc, CHUNK, G, N)
    dA = dtc * A.astype(f32).reshape(G, R)
    cs = jnp.cumsum(dA, axis=2)
    seg = cs[:, :, :, None] - cs[:, :, None, :]
    mask = jnp.tril(jnp.ones((CHUNK, CHUNK), dtype=bool))[:, :, None, None]
    decay = jnp.exp(jnp.where(mask, seg, -jnp.inf))
    CB = jnp.einsum("bclgn,bcsgn->bclsg", Cc, Bc)
    y_diag = jnp.einsum("bclsg,bclsgr,bcsgr,bcsgrp->bclgrp", CB, decay, dtc, xc)
    decay_to_end = jnp.exp(cs[:, :, -1:] - cs)
    states = jnp.einsum("bclgn,bclgr,bclgrp->bcgrpn", Bc, decay_to_end * dtc, xc)
    chunk_decay = jnp.exp(cs[:, :, -1])

    def step(h, inp):
        s, d = inp
        return h * d[..., None, None] + s, h

    h_final, h_starts = lax.scan(step, h0.astype(f32).reshape(b, G, R, P, N),
                                 (jnp.moveaxis(states, 1, 0), jnp.moveaxis(chunk_decay, 1, 0)))
    h_starts = jnp.moveaxis(h_starts, 0, 1)
    y_off = jnp.einsum("bclgn,bcgrpn,bclgr->bclgrp", Cc, h_starts, jnp.exp(cs))
    y = (y_diag + y_off).reshape(b, L, H, P)
    return y.astype(x.dtype), h_final.reshape(b, H, P, N)


def ssd_mixer(h, n_seg, h0_f, h0_b, w_in, conv_w, conv_b, dt_bias, a_log, d_skip, norm_w, w_out):
    b, L, _ = h.shape
    proj = h @ w_in
    z = proj[..., :D_INNER]
    xbc = proj[..., D_INNER:D_INNER + D_XBC]
    dt_raw = proj[..., D_INNER + D_XBC:].reshape(b, L, 2, N_HEADS)
    xbc = jax.nn.silu(centred_dwconv(xbc, conv_w, conv_b, n_seg))
    xs = xbc[..., :D_INNER].reshape(b, L, N_HEADS, HEAD_DIM)
    Bm = xbc[..., D_INNER:D_INNER + N_GROUPS * D_STATE].reshape(b, L, N_GROUPS, D_STATE)
    Cm = xbc[..., D_INNER + N_GROUPS * D_STATE:].reshape(b, L, N_GROUPS, D_STATE)
    dt = jax.nn.softplus((dt_raw + dt_bias).astype(jnp.float32))
    A = -jnp.exp(a_log.astype(jnp.float32))
    y_f, hf = ssd_scan(xs, dt[:, :, 0], A[0], Bm, Cm, h0_f)
    y_b, hb = ssd_scan(jnp.flip(xs, 1), jnp.flip(dt[:, :, 1], 1), A[1],
                       jnp.flip(Bm, 1), jnp.flip(Cm, 1), h0_b)
    y = y_f + jnp.flip(y_b, 1) + d_skip[:, None] * xs
    y = y.reshape(b, L, D_INNER) * jax.nn.silu(z)
    return rmsnorm(y, norm_w) @ w_out, hf, hb


def fourier_mixer(h, w_out, b_out):
    b, L, D = h.shape
    hg = h.astype(jnp.float32).reshape(b, L, N_FGROUPS, D // N_FGROUPS)
    f = jnp.fft.fft2(hg, axes=(1, 3), norm="ortho").real
    return f.reshape(b, L, D).astype(h.dtype) @ w_out + b_out


def sq_relu_mlp(h, w1, w2):
    return jnp.square(jax.nn.relu(h @ w1)) @ w2


def setup_inputs(seed: int = 0) -> dict:
    key = jax.random.key(seed)
    ks = jax.random.split(key, 24)
    nrm = jax.random.normal
    dt0 = jnp.exp(jax.random.uniform(ks[0], (N_SSD, 2, N_HEADS),
                                     minval=math.log(1e-3), maxval=math.log(1e-1)))
    dt_bias = dt0 + jnp.log(-jnp.expm1(-dt0))
    a_log = jnp.log(jax.random.uniform(ks[1], (N_SSD, 2, N_HEADS), minval=1.0, maxval=16.0))
    return {
        "x_prompt": nrm(ks[2], (BATCH, SEQ, D_MODEL), jnp.float32),
        "x_sample": nrm(ks[3], (DEC_BATCH, DEC_SEQ, D_MODEL), jnp.float32),
        "state_ssd": 0.5 * nrm(ks[4], (DEC_BATCH, N_SSD, 2, N_HEADS, HEAD_DIM, D_STATE), jnp.float32),
        "c": nrm(ks[5], (DEC_BATCH, D_MODEL), jnp.float32),
        "c_ctx": nrm(ks[6], (D_MODEL,), jnp.float32),
        "ada_w": 0.5 * D_MODEL ** -0.5 * nrm(ks[7], (DEPTH, D_MODEL, 6 * D_MODEL), jnp.float32),
        "ada_b": 0.02 * nrm(ks[8], (DEPTH, 6 * D_MODEL), jnp.float32),
        "norm_mix_w": 1.0 + 0.02 * nrm(ks[9], (DEPTH, D_MODEL), jnp.float32),
        "norm_mlp_w": 1.0 + 0.02 * nrm(ks[10], (DEPTH, D_MODEL), jnp.float32),
        "ssd_w_in": D_MODEL ** -0.5 * nrm(ks[11], (N_SSD, D_MODEL, D_IN_PROJ), jnp.float32),
        "ssd_conv_w": CONV_W ** -0.5 * nrm(ks[12], (N_SSD, CONV_W, D_XBC), jnp.float32),
        "ssd_conv_b": 0.02 * nrm(ks[13], (N_SSD, D_XBC), jnp.float32),
        "ssd_dt_bias": dt_bias,
        "ssd_a_log": a_log,
        "ssd_d": 1.0 + 0.1 * nrm(ks[14], (N_SSD, N_HEADS), jnp.float32),
        "ssd_norm_w": 1.0 + 0.02 * nrm(ks[15], (N_SSD, D_INNER), jnp.float32),
        "ssd_w_out": D_INNER ** -0.5 * nrm(ks[16], (N_SSD, D_INNER, D_MODEL), jnp.float32),
        "fno_w_out": D_MODEL ** -0.5 * nrm(ks[17], (N_FNO, D_MODEL, D_MODEL), jnp.float32),
        "fno_b_out": 0.02 * nrm(ks[18], (N_FNO, D_MODEL), jnp.float32),
        "mlp_w1": D_MODEL ** -0.5 * nrm(ks[19], (DEPTH, D_MODEL, D_FF), jnp.float32),
        "mlp_w2": D_FF ** -0.5 * nrm(ks[20], (DEPTH, D_FF, D_MODEL), jnp.float32),
        "final_norm_w": 1.0 + 0.02 * nrm(ks[21], (D_MODEL,), jnp.float32),
    }


def reference(x_prompt, x_sample, state_ssd, c, c_ctx, ada_w, ada_b, norm_mix_w, norm_mlp_w,
              ssd_w_in, ssd_conv_w, ssd_conv_b, ssd_dt_bias, ssd_a_log, ssd_d, ssd_norm_w,
              ssd_w_out, fno_w_out, fno_b_out, mlp_w1, mlp_w2, final_norm_w):
    rows = x_sample.shape[1] // GRID_W
    b_ctx = x_prompt.shape[0]
    xp, xs = x_prompt, x_sample
    zeros_state = jnp.zeros((b_ctx, N_HEADS, HEAD_DIM, D_STATE), jnp.float32)
    new_states = []
    for i in range(DEPTH):
        sm_p, cm_p, gm_p, sf_p, cf_p, gf_p = ada_modulation(c_ctx[None], ada_w[i], ada_b[i])
        sm_s, cm_s, gm_s, sf_s, cf_s, gf_s = ada_modulation(c, ada_w[i], ada_b[i])
        hp = rmsnorm(xp, norm_mix_w[i]) * (1.0 + cm_p) + sm_p
        hs = rmsnorm(xs, norm_mix_w[i]) * (1.0 + cm_s) + sm_s
        if i % N_MIXERS == 0:
            j = i // N_MIXERS
            prm = (ssd_w_in[j], ssd_conv_w[j], ssd_conv_b[j], ssd_dt_bias[j], ssd_a_log[j],
                   ssd_d[j], ssd_norm_w[j], ssd_w_out[j])
            op, hf, hb = ssd_mixer(hp, 1, zeros_state, zeros_state, *prm)
            new_states.append(jnp.stack([hf, hb], axis=1).astype(x_prompt.dtype))
            os_, _, _ = ssd_mixer(hs, rows, state_ssd[:, j, 0], state_ssd[:, j, 1], *prm)
        else:
            j = i // N_MIXERS
            op = fourier_mixer(hp, fno_w_out[j], fno_b_out[j])
            os_ = fourier_mixer(hs, fno_w_out[j], fno_b_out[j])
        xp = xp + gm_p * op
        xs = xs + gm_s * os_
        hp = rmsnorm(xp, norm_mlp_w[i]) * (1.0 + cf_p) + sf_p
        hs = rmsnorm(xs, norm_mlp_w[i]) * (1.0 + cf_s) + sf_s
        xp = xp + gf_p * sq_relu_mlp(hp, mlp_w1[i], mlp_w2[i])
        xs = xs + gf_s * sq_relu_mlp(hs, mlp_w1[i], mlp_w2[i])
    new_state_ssd = jnp.stack(new_states, axis=1)
    y_prompt = rmsnorm(xp, final_norm_w)
    y_sample = rmsnorm(xs, final_norm_w)
    return (y_prompt, y_sample, new_state_ssd)
```

```python
import functools

import numpy as np
import jax
import jax.numpy as jnp
from jax import lax
from jax.experimental import pallas as pl
from jax.experimental.pallas import tpu as pltpu

D_MODEL = 1024
DEPTH = 4
GRID_W = 64
D_INNER = 2048
HEAD_DIM = 64
N_HEADS = 32
N_GROUPS = 4
HEADS_PER_GROUP = N_HEADS // N_GROUPS
D_STATE = 128
CHUNK = 128
D_BC = N_GROUPS * D_STATE
D_XBC = D_INNER + 2 * D_BC
N_FGROUPS = 4
D_FG = D_MODEL // N_FGROUPS
D_FF = 4 * D_MODEL
EPS = 1e-6
N_COND = 8

LANES = 128
TOKEN_TILE = 256
XBC_TILE = 512
FF_TILE = 512
ADA_TILE = 1536
VMEM_LIMIT = 56 * 1024 * 1024

LOG2E = 1.4426950408889634

F32 = jnp.float32
BF16 = jnp.bfloat16


def _dot(a, b):
    return jnp.dot(a, b, preferred_element_type=F32)


def _dot_nt(a, b):
    return lax.dot_general(a, b, (((1,), (1,)), ((), ())), preferred_element_type=F32)


def _sigmoid(x):
    return 1.0 / (1.0 + jnp.exp(-x))


def _rms(x):
    return x * lax.rsqrt(jnp.mean(x * x, axis=-1, keepdims=True) + EPS)


def _const_spec(shape):
    nd = len(shape)
    return pl.BlockSpec(shape, lambda *_: (0,) * nd, pipeline_mode=pl.Buffered(1))


def _mod_spec(layer, tiles_per_cond, first_row):
    return pl.BlockSpec((None, None, 6, D_MODEL),
                        lambda t, *_: (layer, first_row + t // tiles_per_cond, 0, 0))


def _params(n_axes=1):
    return pltpu.CompilerParams(dimension_semantics=("parallel",) * n_axes,
                                vmem_limit_bytes=VMEM_LIMIT)


def _ada_kernel(c_ref, w_ref, b_ref, o_ref):
    c = c_ref[...]
    s = (c * _sigmoid(c)).astype(BF16)
    o_ref[...] = _dot(s, w_ref[...].astype(BF16)) + b_ref[...]


def _ada_modulation(cond, ada_w, ada_b):
    n_out = 6 * D_MODEL
    out = pl.pallas_call(
        _ada_kernel,
        grid=(DEPTH, n_out // ADA_TILE),
        in_specs=[
            pl.BlockSpec((N_COND, D_MODEL), lambda i, n: (0, 0)),
            pl.BlockSpec((None, D_MODEL, ADA_TILE), lambda i, n: (i, 0, n)),
            pl.BlockSpec((None, 1, ADA_TILE), lambda i, n: (i, 0, n)),
        ],
        out_specs=pl.BlockSpec((None, N_COND, ADA_TILE), lambda i, n: (i, 0, n)),
        out_shape=jax.ShapeDtypeStruct((DEPTH, N_COND, n_out), F32),
        compiler_params=_params(2),
        name="ada_mod",
    )(cond, ada_w, ada_b.reshape(DEPTH, 1, n_out))
    return out.reshape(DEPTH, N_COND, 6, D_MODEL)


def _ssd_in_kernel(x_ref, mod_ref, nw_ref, wz_ref, wxbc_ref, wdt_ref, cw_ref, cb_ref, dtb_ref,
                   z_ref, xs_ref, b_ref, c_ref, dt_ref, *, seg):
    tm = x_ref.shape[0]
    h = _rms(x_ref[...]) * nw_ref[...]
    h = h * (1.0 + mod_ref[1:2, :]) + mod_ref[0:1, :]
    hb = h.astype(BF16)
    z_ref[...] = _dot(hb, wz_ref[...]).astype(BF16)

    row = lax.broadcasted_iota(jnp.int32, (tm, XBC_TILE), 0) & (seg - 1)
    first = row == 0
    last = row == seg - 1
    for j in range(D_XBC // XBC_TILE):
        cols = slice(j * XBC_TILE, (j + 1) * XBC_TILE)
        acc = _dot(hb, wxbc_ref[:, cols])
        up = jnp.where(first, 0.0, pltpu.roll(acc, 1, axis=0))
        dn = jnp.where(last, 0.0, pltpu.roll(acc, tm - 1, axis=0))
        o = up * cw_ref[0:1, cols] + acc * cw_ref[1:2, cols] + dn * cw_ref[2:3, cols] + cb_ref[:, cols]
        o = (o * _sigmoid(o)).astype(BF16)
        if j < D_INNER // XBC_TILE:
            xs_ref[:, cols] = o
        elif j == D_INNER // XBC_TILE:
            b_ref[...] = o
        else:
            c_ref[...] = o

    dtT = _dot_nt(wdt_ref[...], hb)
    for k in range(tm // CHUNK):
        v = dtT[:, k * CHUNK:(k + 1) * CHUNK] + dtb_ref[...]
        dt_ref[k] = jnp.maximum(v, 0.0) + jnp.log1p(jnp.exp(-jnp.abs(v)))


def _ssd_in(x, mods, layer, first_row, tiles_per_cond, seg, nw, wz, wxbc, wdt, cw, cb, dtb):
    t_tokens = x.shape[0]
    tm = TOKEN_TILE
    tok = lambda n: pl.BlockSpec((tm, n), lambda t: (t, 0))
    return pl.pallas_call(
        functools.partial(_ssd_in_kernel, seg=seg),
        grid=(t_tokens // tm,),
        in_specs=[
            tok(D_MODEL),
            _mod_spec(layer, tiles_per_cond, first_row),
            _const_spec((1, D_MODEL)),
            _const_spec((D_MODEL, D_INNER)),
            _const_spec((D_MODEL, D_XBC)),
            _const_spec((2 * N_HEADS, D_MODEL)),
            _const_spec((3, D_XBC)),
            _const_spec((1, D_XBC)),
            _const_spec((2 * N_HEADS, LANES)),
        ],
        out_specs=[
            tok(D_INNER), tok(D_INNER), tok(D_BC), tok(D_BC),
            pl.BlockSpec((tm // CHUNK, 2 * N_HEADS, LANES), lambda t: (t, 0, 0)),
        ],
        out_shape=[
            jax.ShapeDtypeStruct((t_tokens, D_INNER), BF16),
            jax.ShapeDtypeStruct((t_tokens, D_INNER), BF16),
            jax.ShapeDtypeStruct((t_tokens, D_BC), BF16),
            jax.ShapeDtypeStruct((t_tokens, D_BC), BF16),
            jax.ShapeDtypeStruct((t_tokens // CHUNK, 2 * N_HEADS, LANES), F32),
        ],
        compiler_params=_params(1),
        name="ssd_in",
    )(x, mods, nw, wz, wxbc, wdt, cw, cb, dtb)


def _split3(a):
    hi = a.astype(BF16)
    r = a - hi.astype(F32)
    mid = r.astype(BF16)
    lo = (r - mid.astype(F32)).astype(BF16)
    return hi, mid, lo


def _dot3(a, m):
    hi, mid, lo = _split3(a)
    return _dot(hi, m) + _dot(mid, m) + _dot(lo, m)


def _ssd_scan_kernel(*refs, seq_len, has_h0, emit_state, has_alias):
    xs_ref, b_ref, c_ref, dt_ref, alog_ref, dskip_ref = refs[:6]
    pos = 6
    h0_ref = refs[pos] if has_h0 else None
    pos += int(has_h0) + int(has_alias)
    y_ref = refs[pos]
    hf_ref = refs[pos + 1] if emit_state else None
    ht_ref = refs[-1]

    nc = seq_len // CHUNK
    hg = HEADS_PER_GROUP
    rows = lax.broadcasted_iota(jnp.int32, (CHUNK, CHUNK), 0)
    lanes = lax.broadcasted_iota(jnp.int32, (CHUNK, CHUNK), 1)
    tril = rows >= lanes
    triu = rows <= lanes
    tril_b = jnp.where(tril, 1.0, 0.0).astype(BF16)
    triu_b = jnp.where(triu, 1.0, 0.0).astype(BF16)
    ones_b = jnp.ones((CHUNK, CHUNK), BF16)
    low_half = lanes < HEAD_DIM
    low_b = jnp.where(low_half, 1.0, 0.0).astype(BF16)
    high_b = jnp.where(low_half, 0.0, 1.0).astype(BF16)
    a2 = -jnp.exp(alog_ref[...]) * LOG2E

    for d in range(2):
        for q in range(hg // 2):
            sl = slice(q * LANES, (q + 1) * LANES)
            if has_h0:
                ht_ref[d, :, sl] = h0_ref[d, sl, :].T
            else:
                ht_ref[d, :, sl] = jnp.zeros((D_STATE, LANES), F32)

        mask = tril if d == 0 else triu
        cum_b = triu_b if d == 0 else tril_b

        def chunk_body(i, carry, d=d, mask=mask, cum_b=cum_b):
            c = i if d == 0 else nc - 1 - i
            t0 = pl.multiple_of(c * CHUNK, CHUNK)
            dt16 = dt_ref[c]
            da16 = dt16 * a2
            cs_t = _dot3(da16, cum_b)[d * hg:(d + 1) * hg]
            tot = _dot3(da16, ones_b)[d * hg:(d + 1) * hg]
            dt8 = dt16[d * hg:(d + 1) * hg]
            da8 = da16[d * hg:(d + 1) * hg]
            w_t = dt8 * jnp.exp2(tot - cs_t)
            cd = jnp.exp2(tot)
            src_t = cs_t - jnp.log(dt8) * LOG2E

            bc = b_ref[pl.ds(t0, CHUNK), :]
            cb16 = c_ref[pl.ds(t0, CHUNK), :]
            cb = _dot_nt(cb16, bc)
            b_t = bc.astype(F32).T

            for q in range(hg // 2):
                sl = slice(q * LANES, (q + 1) * LANES)
                xpb = xs_ref[pl.ds(t0, CHUNK), sl]
                x2 = jnp.concatenate([xpb * low_b, xpb * high_b], axis=0)
                htp = ht_ref[d, :, sl]
                y_off = _dot(cb16, htp.astype(BF16))
                m_parts, bw_parts, e_parts = [], [], []
                for r in (2 * q, 2 * q + 1):
                    cs_col = jnp.sum(jnp.where(mask, da8[r:r + 1, :], 0.0), axis=1, keepdims=True)
                    decay = jnp.where(mask, jnp.exp2(cs_col - src_t[r:r + 1, :]), 0.0)
                    m_parts.append((cb * decay).astype(BF16))
                    bw_parts.append((b_t * w_t[r:r + 1, :]).astype(BF16))
                    e_parts.append(jnp.exp2(cs_col))
                y_d = _dot(jnp.concatenate(m_parts, axis=1), x2)
                st = _dot(jnp.concatenate(bw_parts, axis=1), x2)
                ecs = jnp.where(low_half, e_parts[0], e_parts[1])
                cdp = jnp.where(low_half, cd[2 * q:2 * q + 1, :], cd[2 * q + 1:2 * q + 2, :])
                y = y_d + y_off * ecs
                ht_ref[d, :, sl] = htp * cdp + st
                if d == 0:
                    y = y + xpb.astype(F32) * dskip_ref[:, sl]
                else:
                    y = y + y_ref[pl.ds(t0, CHUNK), sl].astype(F32)
                y_ref[pl.ds(t0, CHUNK), sl] = y.astype(BF16)
            return carry

        lax.fori_loop(0, nc, chunk_body, 0)

        if emit_state:
            for q in range(hg // 2):
                sl = slice(q * LANES, (q + 1) * LANES)
                hf_ref[d, sl, :] = ht_ref[d, :, sl].T


def _ssd_scan(xs, bm, cm, dt, alog, dskip, seq_len, ssd_index, h0=None, state_out=None):
    t_tokens = xs.shape[0]
    nb = t_tokens // seq_len
    gw = HEADS_PER_GROUP * HEAD_DIM
    nc = seq_len // CHUNK
    in_specs = [
        pl.BlockSpec((seq_len, gw), lambda b, g: (b, g)),
        pl.BlockSpec((seq_len, D_STATE), lambda b, g: (b, g)),
        pl.BlockSpec((seq_len, D_STATE), lambda b, g: (b, g)),
        pl.BlockSpec((nc, 2 * HEADS_PER_GROUP, LANES), lambda b, g: (b, g, 0)),
        pl.BlockSpec((2 * HEADS_PER_GROUP, LANES), lambda b, g: (g, 0)),
        pl.BlockSpec((1, gw), lambda b, g: (0, g)),
    ]
    args = [xs, bm, cm, dt, alog, dskip]
    has_h0 = h0 is not None
    if has_h0:
        in_specs.append(pl.BlockSpec((None, None, 2, None, gw, D_STATE),
                                     lambda b, g: (b, ssd_index, 0, g, 0, 0)))
        args.append(h0)
    emit_state = state_out is not None
    has_alias = emit_state and not isinstance(state_out, jax.ShapeDtypeStruct)
    out_specs = [pl.BlockSpec((seq_len, gw), lambda b, g: (b, g))]
    out_shape = [jax.ShapeDtypeStruct((t_tokens, D_INNER), BF16)]
    aliases = {}
    if emit_state:
        if has_alias:
            in_specs.append(pl.BlockSpec(memory_space=pl.ANY))
            args.append(state_out)
            aliases = {len(args) - 1: 1}
        state_sds = jax.ShapeDtypeStruct(state_out.shape, state_out.dtype)
        out_specs.append(pl.BlockSpec((None, None, 2, None, gw, D_STATE),
                                      lambda b, g: (b, ssd_index, 0, g, 0, 0)))
        out_shape.append(state_sds)
    res = pl.pallas_call(
        functools.partial(_ssd_scan_kernel, seq_len=seq_len, has_h0=has_h0, emit_state=emit_state,
                          has_alias=has_alias),
        grid=(nb, N_GROUPS),
        in_specs=in_specs,
        out_specs=out_specs,
        out_shape=out_shape,
        scratch_shapes=[pltpu.VMEM((2, D_STATE, gw), F32)],
        input_output_aliases=aliases,
        compiler_params=_params(2),
        name="ssd_scan",
    )(*args)
    return res if emit_state else (res[0], None)


def _ssd_out_kernel(y_ref, z_ref, x_ref, mod_ref, nw_ref, wo_ref, o_ref):
    z = z_ref[...].astype(F32)
    g = y_ref[...].astype(F32) * (z * _sigmoid(z))
    gn = (_rms(g) * nw_ref[...]).astype(BF16)
    o_ref[...] = x_ref[...] + mod_ref[2:3, :] * _dot(gn, wo_ref[...])


def _ssd_out(y, z, x, mods, layer, first_row, tiles_per_cond, nw, wo):
    t_tokens = x.shape[0]
    tm = TOKEN_TILE
    tok = lambda n: pl.BlockSpec((tm, n), lambda t: (t, 0))
    return pl.pallas_call(
        _ssd_out_kernel,
        grid=(t_tokens // tm,),
        in_specs=[tok(D_INNER), tok(D_INNER), tok(D_MODEL),
                  _mod_spec(layer, tiles_per_cond, first_row),
                  _const_spec((1, D_INNER)), _const_spec((D_INNER, D_MODEL))],
        out_specs=tok(D_MODEL),
        out_shape=jax.ShapeDtypeStruct((t_tokens, D_MODEL), F32),
        compiler_params=_params(1),
        name="ssd_out",
    )(y, z, x, mods, nw, wo)


def _mlp_kernel(x_ref, mod_ref, nw_ref, w1_ref, w2_ref, fw_ref, o_ref, *, final):
    x = x_ref[...]
    h = _rms(x) * nw_ref[...]
    hb = (h * (1.0 + mod_ref[4:5, :]) + mod_ref[3:4, :]).astype(BF16)
    acc = jnp.zeros(x.shape, F32)
    for k in range(D_FF // FF_TILE):
        a = jnp.maximum(_dot(hb, w1_ref[:, k * FF_TILE:(k + 1) * FF_TILE]), 0.0)
        acc = acc + _dot((a * a).astype(BF16), w2_ref[k * FF_TILE:(k + 1) * FF_TILE, :])
    x2 = x + mod_ref[5:6, :] * acc
    if final:
        x2 = _rms(x2) * fw_ref[...]
    o_ref[...] = x2


def _mlp(x, mods, layer, first_row, tiles_per_cond, nw, w1, w2, fw, final):
    t_tokens = x.shape[0]
    tm = TOKEN_TILE
    tok = pl.BlockSpec((tm, D_MODEL), lambda t: (t, 0))
    return pl.pallas_call(
        functools.partial(_mlp_kernel, final=final),
        grid=(t_tokens // tm,),
        in_specs=[tok, _mod_spec(layer, tiles_per_cond, first_row), _const_spec((1, D_MODEL)),
                  _const_spec((D_MODEL, D_FF)), _const_spec((D_FF, D_MODEL)), _const_spec((1, D_MODEL))],
        out_specs=tok,
        out_shape=jax.ShapeDtypeStruct((t_tokens, D_MODEL), F32),
        compiler_params=_params(1),
        name="mlp",
    )(x, mods, nw, w1, w2, fw)


@functools.lru_cache(maxsize=None)
def _dft_matrices(n):
    k = np.arange(n)
    ang = 2.0 * np.pi * ((k[:, None] * k[None, :]) % n) / n
    return np.cos(ang) / np.sqrt(n), np.sin(ang) / np.sqrt(n)


def _fnet_kernel(x_ref, mod_ref, nw_ref, cc_ref, sc_ref, cl_ref, sl_ref, wf_ref, bf_ref, o_ref, f_ref):
    x = x_ref[...]
    h = _rms(x) * nw_ref[...]
    hb = (h * (1.0 + mod_ref[1:2, :]) + mod_ref[0:1, :]).astype(BF16)
    for g in range(N_FGROUPS):
        cols = slice(g * D_FG, (g + 1) * D_FG)
        hg = hb[:, cols]
        pc = _dot(hg, cc_ref[...]).astype(BF16)
        ps = _dot(hg, sc_ref[...]).astype(BF16)
        f_ref[:, cols] = (_dot(cl_ref[...], pc) - _dot(sl_ref[...], ps)).astype(BF16)
    o_ref[...] = x + mod_ref[2:3, :] * (_dot(f_ref[...], wf_ref[...]) + bf_ref[...])


def _fnet(x, mods, layer, first_row, seqs_per_cond, seq_len, nw, wf, bf):
    t_tokens = x.shape[0]
    cc, sc = (jnp.asarray(m, F32).astype(BF16) for m in _dft_matrices(D_FG))
    cl, sl = (jnp.asarray(m, F32).astype(BF16) for m in _dft_matrices(seq_len))
    tok = pl.BlockSpec((seq_len, D_MODEL), lambda t: (t, 0))
    return pl.pallas_call(
        _fnet_kernel,
        grid=(t_tokens // seq_len,),
        in_specs=[tok, _mod_spec(layer, seqs_per_cond, first_row), _const_spec((1, D_MODEL)),
                  _const_spec((D_FG, D_FG)), _const_spec((D_FG, D_FG)),
                  _const_spec((seq_len, seq_len)), _const_spec((seq_len, seq_len)),
                  _const_spec((D_MODEL, D_MODEL)), _const_spec((1, D_MODEL))],
        out_specs=tok,
        out_shape=jax.ShapeDtypeStruct((t_tokens, D_MODEL), F32),
        scratch_shapes=[pltpu.VMEM((seq_len, D_MODEL), BF16)],
        compiler_params=_params(1),
        name="fnet",
    )(x, mods, nw, cc, sc, cl, sl, wf, bf)


def _head_perm():
    return np.array([d * N_HEADS + g * HEADS_PER_GROUP + r
                     for g in range(N_GROUPS) for d in range(2) for r in range(HEADS_PER_GROUP)])


def kernel(x_prompt, x_sample, state_ssd, c, c_ctx, ada_w, ada_b, norm_mix_w, norm_mlp_w, ssd_w_in, ssd_conv_w, ssd_conv_b, ssd_dt_bias, ssd_a_log, ssd_d, ssd_norm_w, ssd_w_out, fno_w_out, fno_b_out, mlp_w1, mlp_w2, final_norm_w):
    batch, seq, _ = x_prompt.shape
    dec_batch, dec_seq, _ = x_sample.shape
    n_ssd = ssd_w_in.shape[0]

    cond = jnp.concatenate([c_ctx[None], c, jnp.zeros((N_COND - 1 - dec_batch, D_MODEL), F32)], axis=0)
    mods = _ada_modulation(cond, ada_w, ada_b)

    xp = x_prompt.reshape(batch * seq, D_MODEL)
    xs = x_sample.reshape(dec_batch * dec_seq, D_MODEL)
    streams = [
        dict(first_row=0, tiles_per_cond=batch * seq // TOKEN_TILE, seqs_per_cond=batch, seg=seq, seq_len=seq),
        dict(first_row=1, tiles_per_cond=dec_seq // TOKEN_TILE, seqs_per_cond=1, seg=GRID_W, seq_len=dec_seq),
    ]
    perm = _head_perm()
    row = lambda v: v.reshape(1, -1)
    state_view = (batch, n_ssd, 2, N_GROUPS, HEADS_PER_GROUP * HEAD_DIM, D_STATE)
    new_state = jax.ShapeDtypeStruct(state_view, F32)
    h0_all = state_ssd.reshape(dec_batch, n_ssd, 2, N_GROUPS, HEADS_PER_GROUP * HEAD_DIM, D_STATE)

    acts = [xp, xs]
    for i in range(DEPTH):
        j = i // 2
        final = i == DEPTH - 1
        if i % 2 == 0:
            w_in = ssd_w_in[j].astype(BF16)
            wz = w_in[:, :D_INNER]
            wxbc = w_in[:, D_INNER:D_INNER + D_XBC]
            wdt = w_in[:, D_INNER + D_XBC:][:, perm].T
            dtb = jnp.broadcast_to(ssd_dt_bias[j].reshape(-1)[perm][:, None], (2 * N_HEADS, LANES))
            alog = jnp.broadcast_to(ssd_a_log[j].reshape(-1)[perm][:, None], (2 * N_HEADS, LANES))
            dskip = row(jnp.repeat(ssd_d[j], HEAD_DIM))
            wo = ssd_w_out[j].astype(BF16)
        else:
            wf = fno_w_out[j].astype(BF16)
        w1 = mlp_w1[i].astype(BF16)
        w2 = mlp_w2[i].astype(BF16)

        for s, st in enumerate(streams):
            x = acts[s]
            fr, tpc = st["first_row"], st["tiles_per_cond"]
            if i % 2 == 0:
                z, xc, bm, cm, dt = _ssd_in(x, mods, i, fr, tpc, st["seg"], row(norm_mix_w[i]), wz, wxbc, wdt,
                                            ssd_conv_w[j], row(ssd_conv_b[j]), dtb)
                if s == 0:
                    y, new_state = _ssd_scan(xc, bm, cm, dt, alog, dskip, st["seq_len"], j, state_out=new_state)
                else:
                    y, _ = _ssd_scan(xc, bm, cm, dt, alog, dskip, st["seq_len"], j, h0=h0_all)
                x = _ssd_out(y, z, x, mods, i, fr, tpc, row(ssd_norm_w[j]), wo)
            else:
                x = _fnet(x, mods, i, fr, st["seqs_per_cond"], st["seq_len"], row(norm_mix_w[i]), wf,
                          row(fno_b_out[j]))
            acts[s] = _mlp(x, mods, i, fr, tpc, row(norm_mlp_w[i]), w1, w2, row(final_norm_w), final)

    y_prompt = acts[0].reshape(batch, seq, D_MODEL)
    y_sample = acts[1].reshape(dec_batch, dec_seq, D_MODEL)
    new_state_ssd = new_state.reshape(batch, n_ssd, 2, N_HEADS, HEAD_DIM, D_STATE)
    return (y_prompt, y_sample, new_state_ssd)
```

```python
import functools

import numpy as np
import jax
import jax.numpy as jnp
from jax import lax
from jax.experimental import pallas as pl
from jax.experimental.pallas import tpu as pltpu

D_MODEL = 1024
DEPTH = 4
GRID_W = 64
D_INNER = 2048
HEAD_DIM = 64
N_HEADS = 32
N_GROUPS = 4
HEADS_PER_GROUP = N_HEADS // N_GROUPS
D_STATE = 128
CHUNK = 128
D_BC = N_GROUPS * D_STATE
D_XBC = D_INNER + 2 * D_BC
D_ZX = D_INNER + D_XBC
D_IN_PROJ = D_ZX + 2 * N_HEADS
N_FGROUPS = 4
D_FG = D_MODEL // N_FGROUPS
D_FF = 4 * D_MODEL
EPS = 1e-6
N_COND = 8
LOG2E = 1.4426950408889634

LANES = 128
TOKEN_TILE = 256
MLP_TILE = 1024
XBC_TILE = 512
FF_TILE = 512
ADA_TILE = 1536
VMEM_LIMIT = 56 * 1024 * 1024

F32 = jnp.float32
BF16 = jnp.bfloat16


def _dot(a, b):
    return jnp.dot(a, b, preferred_element_type=F32)


def _dot_nt(a, b):
    return lax.dot_general(a, b, (((1,), (1,)), ((), ())), preferred_element_type=F32)


def _sigmoid(x):
    return 1.0 / (1.0 + jnp.exp(-x))


def _rms(x):
    return x * lax.rsqrt(jnp.mean(x * x, axis=-1, keepdims=True) + EPS)


def _const_spec(shape):
    nd = len(shape)
    return pl.BlockSpec(shape, lambda *_: (0,) * nd, pipeline_mode=pl.Buffered(1))


def _layer_spec(layer, tail):
    nd = len(tail)
    return pl.BlockSpec((None,) + tuple(tail), lambda *_: (layer,) + (0,) * nd,
                        pipeline_mode=pl.Buffered(1))


def _mod_spec(layer, tiles_per_cond, first_row):
    return pl.BlockSpec((None, None, 6, D_MODEL),
                        lambda t, *_: (layer, first_row + t // tiles_per_cond, 0, 0))


def _params(*semantics):
    return pltpu.CompilerParams(dimension_semantics=semantics, vmem_limit_bytes=VMEM_LIMIT)


def _cast_weight(dst_ref, src_ref, n_cols, tile=512):
    for j in range(0, n_cols, tile):
        dst_ref[:, j:j + tile] = src_ref[:, j:j + tile].astype(BF16)


def _ada_kernel(c_ref, w_ref, b_ref, o_ref):
    c = c_ref[...]
    s = (c * _sigmoid(c)).astype(BF16)
    o_ref[...] = _dot(s, w_ref[...].astype(BF16)) + b_ref[...]


def _ada_modulation(cond, ada_w, ada_b):
    n_out = 6 * D_MODEL
    out = pl.pallas_call(
        _ada_kernel,
        grid=(DEPTH, n_out // ADA_TILE),
        in_specs=[
            pl.BlockSpec((N_COND, D_MODEL), lambda i, n: (0, 0)),
            pl.BlockSpec((None, D_MODEL, ADA_TILE), lambda i, n: (i, 0, n)),
            pl.BlockSpec((None, 1, ADA_TILE), lambda i, n: (i, 0, n)),
        ],
        out_specs=pl.BlockSpec((None, N_COND, ADA_TILE), lambda i, n: (i, 0, n)),
        out_shape=jax.ShapeDtypeStruct((DEPTH, N_COND, n_out), F32),
        compiler_params=_params("parallel", "parallel"),
        name="ada_mod",
    )(cond, ada_w, ada_b.reshape(DEPTH, 1, n_out))
    return out.reshape(DEPTH, N_COND, 6, D_MODEL)


def _ssd_in_kernel(x_ref, mod_ref, nw_ref, win_ref, wdt_ref, cw_ref, cb_ref, dtb_ref,
                   z_ref, xs_ref, b_ref, c_ref, dt_ref, wbf_ref, *, seg):
    @pl.when(pl.program_id(0) == 0)
    def _():
        _cast_weight(wbf_ref, win_ref, D_ZX)

    tm = x_ref.shape[0]
    h = _rms(x_ref[...]) * nw_ref[...]
    h = h * (1.0 + mod_ref[1:2, :]) + mod_ref[0:1, :]
    hb = h.astype(BF16)
    z_ref[...] = _dot(hb, wbf_ref[:, :D_INNER]).astype(BF16)

    row = lax.broadcasted_iota(jnp.int32, (tm, XBC_TILE), 0) & (seg - 1)
    first = row == 0
    last = row == seg - 1
    for j in range(D_XBC // XBC_TILE):
        cols = slice(j * XBC_TILE, (j + 1) * XBC_TILE)
        acc = _dot(hb, wbf_ref[:, D_INNER + j * XBC_TILE:D_INNER + (j + 1) * XBC_TILE])
        up = jnp.where(first, 0.0, pltpu.roll(acc, 1, axis=0))
        dn = jnp.where(last, 0.0, pltpu.roll(acc, tm - 1, axis=0))
        o = up * cw_ref[0:1, cols] + acc * cw_ref[1:2, cols] + dn * cw_ref[2:3, cols] + cb_ref[:, cols]
        o = (o * _sigmoid(o)).astype(BF16)
        if j < D_INNER // XBC_TILE:
            xs_ref[:, cols] = o
        elif j == D_INNER // XBC_TILE:
            b_ref[...] = o
        else:
            c_ref[...] = o

    dtT = _dot_nt(wdt_ref[...], hb)
    for k in range(tm // CHUNK):
        v = dtT[:, k * CHUNK:(k + 1) * CHUNK] + dtb_ref[...]
        dt_ref[k] = jnp.maximum(v, 0.0) + jnp.log1p(jnp.exp(-jnp.abs(v)))


def _ssd_in(x, mods, layer, j, first_row, tokens_per_cond, seg, norm_w, w_in, wdt, conv_w, conv_b, dtb):
    t_tokens = x.shape[0]
    tm = TOKEN_TILE
    tok = lambda n: pl.BlockSpec((tm, n), lambda t: (t, 0))
    return pl.pallas_call(
        functools.partial(_ssd_in_kernel, seg=seg),
        grid=(t_tokens // tm,),
        in_specs=[
            tok(D_MODEL),
            _mod_spec(layer, tokens_per_cond // tm, first_row),
            _layer_spec(layer, (1, D_MODEL)),
            _layer_spec(j, (D_MODEL, D_IN_PROJ)),
            _layer_spec(j, (2 * N_HEADS, D_MODEL)),
            _layer_spec(j, (3, D_XBC)),
            _layer_spec(j, (1, D_XBC)),
            _layer_spec(j, (2 * N_HEADS, LANES)),
        ],
        out_specs=[
            tok(D_INNER), tok(D_INNER), tok(D_BC), tok(D_BC),
            pl.BlockSpec((tm // CHUNK, 2 * N_HEADS, LANES), lambda t: (t, 0, 0)),
        ],
        out_shape=[
            jax.ShapeDtypeStruct((t_tokens, D_INNER), BF16),
            jax.ShapeDtypeStruct((t_tokens, D_INNER), BF16),
            jax.ShapeDtypeStruct((t_tokens, D_BC), BF16),
            jax.ShapeDtypeStruct((t_tokens, D_BC), BF16),
            jax.ShapeDtypeStruct((t_tokens // CHUNK, 2 * N_HEADS, LANES), F32),
        ],
        scratch_shapes=[pltpu.VMEM((D_MODEL, D_ZX), BF16)],
        compiler_params=_params("arbitrary"),
        name="ssd_in",
    )(x, mods, norm_w, w_in, wdt, conv_w, conv_b, dtb)


def _split3(a):
    hi = a.astype(BF16)
    r = a - hi.astype(F32)
    mid = r.astype(BF16)
    lo = (r - mid.astype(F32)).astype(BF16)
    return hi, mid, lo


def _dot3(a, m):
    hi, mid, lo = _split3(a)
    return _dot(hi, m) + _dot(mid, m) + _dot(lo, m)


def _ssd_scan_kernel(*refs, seq_len, has_h0, emit_state, has_alias):
    xs_ref, b_ref, c_ref, dt_ref, alog_ref, dskip_ref = refs[:6]
    pos = 6
    h0_ref = refs[pos] if has_h0 else None
    pos += int(has_h0) + int(has_alias)
    y_ref = refs[pos]
    hf_ref = refs[pos + 1] if emit_state else None
    ht_ref = refs[-1]

    nc = seq_len // CHUNK
    hg = HEADS_PER_GROUP
    rows = lax.broadcasted_iota(jnp.int32, (CHUNK, CHUNK), 0)
    lanes = lax.broadcasted_iota(jnp.int32, (CHUNK, CHUNK), 1)
    tril = rows >= lanes
    triu = rows <= lanes
    tril_b = jnp.where(tril, 1.0, 0.0).astype(BF16)
    triu_b = jnp.where(triu, 1.0, 0.0).astype(BF16)
    ones_b = jnp.ones((CHUNK, CHUNK), BF16)
    low_half = lanes < HEAD_DIM
    low_b = jnp.where(low_half, 1.0, 0.0).astype(BF16)
    high_b = jnp.where(low_half, 0.0, 1.0).astype(BF16)
    a2 = -jnp.exp(alog_ref[...]) * LOG2E

    y_ref[...] = jnp.zeros(y_ref.shape, y_ref.dtype)
    for d in range(2):
        for q in range(hg // 2):
            sl = slice(q * LANES, (q + 1) * LANES)
            if has_h0:
                ht_ref[d, :, sl] = h0_ref[d, sl, :].T
            else:
                ht_ref[d, :, sl] = jnp.zeros((D_STATE, LANES), F32)

    def chunk_step(d, c):
        mask = tril if d == 0 else triu
        cum_b = triu_b if d == 0 else tril_b
        t0 = pl.multiple_of(c * CHUNK, CHUNK)
        dt16 = dt_ref[c]
        da16 = dt16 * a2
        cs_t = _dot3(da16, cum_b)[d * hg:(d + 1) * hg]
        tot = _dot3(da16, ones_b)[d * hg:(d + 1) * hg]
        dt8 = dt16[d * hg:(d + 1) * hg]
        da8 = da16[d * hg:(d + 1) * hg]
        w_t = dt8 * jnp.exp2(tot - cs_t)
        cd = jnp.exp2(tot)
        src_t = cs_t - jnp.log(dt8) * LOG2E

        bc = b_ref[pl.ds(t0, CHUNK), :]
        cb16 = c_ref[pl.ds(t0, CHUNK), :]
        cb = _dot_nt(cb16, bc)
        b_t = bc.astype(F32).T

        for q in range(hg // 2):
            sl = slice(q * LANES, (q + 1) * LANES)
            xpb = xs_ref[pl.ds(t0, CHUNK), sl]
            x2 = jnp.concatenate([xpb * low_b, xpb * high_b], axis=0)
            htp = ht_ref[d, :, sl]
            y_off = _dot(cb16, htp.astype(BF16))
            m_parts, bw_parts, e_parts = [], [], []
            for r in (2 * q, 2 * q + 1):
                cs_col = jnp.sum(jnp.where(mask, da8[r:r + 1, :], 0.0), axis=1, keepdims=True)
                decay = jnp.where(mask, jnp.exp2(cs_col - src_t[r:r + 1, :]), 0.0)
                m_parts.append((cb * decay).astype(BF16))
                bw_parts.append((b_t * w_t[r:r + 1, :]).astype(BF16))
                e_parts.append(jnp.exp2(cs_col))
            y_d = _dot(jnp.concatenate(m_parts, axis=1), x2)
            st = _dot(jnp.concatenate(bw_parts, axis=1), x2)
            ecs = jnp.where(low_half, e_parts[0], e_parts[1])
            cdp = jnp.where(low_half, cd[2 * q:2 * q + 1, :], cd[2 * q + 1:2 * q + 2, :])
            y = y_d + y_off * ecs + y_ref[pl.ds(t0, CHUNK), sl].astype(F32)
            ht_ref[d, :, sl] = htp * cdp + st
            if d == 0:
                y = y + xpb.astype(F32) * dskip_ref[:, sl]
            y_ref[pl.ds(t0, CHUNK), sl] = y.astype(BF16)

    def body(i, carry):
        chunk_step(0, i)
        chunk_step(1, nc - 1 - i)
        return carry

    lax.fori_loop(0, nc, body, 0)

    if emit_state:
        for d in range(2):
            for q in range(hg // 2):
                sl = slice(q * LANES, (q + 1) * LANES)
                hf_ref[d, sl, :] = ht_ref[d, :, sl].T


def _ssd_scan(xs, bm, cm, dt, alog, dskip, seq_len, ssd_index, h0=None, state_out=None):
    t_tokens = xs.shape[0]
    nb = t_tokens // seq_len
    gw = HEADS_PER_GROUP * HEAD_DIM
    nc = seq_len // CHUNK
    in_specs = [
        pl.BlockSpec((seq_len, gw), lambda b, g: (b, g)),
        pl.BlockSpec((seq_len, D_STATE), lambda b, g: (b, g)),
        pl.BlockSpec((seq_len, D_STATE), lambda b, g: (b, g)),
        pl.BlockSpec((nc, 2 * HEADS_PER_GROUP, LANES), lambda b, g: (b, g, 0)),
        pl.BlockSpec((None, 2 * HEADS_PER_GROUP, LANES), lambda b, g: (ssd_index, g, 0)),
        pl.BlockSpec((None, 1, gw), lambda b, g: (ssd_index, 0, g)),
    ]
    args = [xs, bm, cm, dt, alog, dskip]
    has_h0 = h0 is not None
    if has_h0:
        in_specs.append(pl.BlockSpec((None, None, 2, None, gw, D_STATE),
                                     lambda b, g: (b, ssd_index, 0, g, 0, 0)))
        args.append(h0)
    emit_state = state_out is not None
    has_alias = emit_state and not isinstance(state_out, jax.ShapeDtypeStruct)
    out_specs = [pl.BlockSpec((seq_len, gw), lambda b, g: (b, g))]
    out_shape = [jax.ShapeDtypeStruct((t_tokens, D_INNER), BF16)]
    aliases = {}
    if emit_state:
        if has_alias:
            in_specs.append(pl.BlockSpec(memory_space=pl.ANY))
            args.append(state_out)
            aliases = {len(args) - 1: 1}
        state_sds = jax.ShapeDtypeStruct(state_out.shape, state_out.dtype)
        out_specs.append(pl.BlockSpec((None, None, 2, None, gw, D_STATE),
                                      lambda b, g: (b, ssd_index, 0, g, 0, 0)))
        out_shape.append(state_sds)
    res = pl.pallas_call(
        functools.partial(_ssd_scan_kernel, seq_len=seq_len, has_h0=has_h0, emit_state=emit_state,
                          has_alias=has_alias),
        grid=(nb, N_GROUPS),
        in_specs=in_specs,
        out_specs=out_specs,
        out_shape=out_shape,
        scratch_shapes=[pltpu.VMEM((2, D_STATE, gw), F32)],
        input_output_aliases=aliases,
        compiler_params=_params("parallel", "parallel"),
        name="ssd_scan",
    )(*args)
    return res if emit_state else (res[0], None)


def _ssd_out_kernel(y_ref, z_ref, x_ref, mod_ref, nw_ref, wo_ref, o_ref, wbf_ref):
    @pl.when(pl.program_id(0) == 0)
    def _():
        _cast_weight(wbf_ref, wo_ref, D_MODEL)

    z = z_ref[...].astype(F32)
    g = y_ref[...].astype(F32) * (z * _sigmoid(z))
    gn = (_rms(g) * nw_ref[...]).astype(BF16)
    o_ref[...] = x_ref[...] + mod_ref[2:3, :] * _dot(gn, wbf_ref[...])


def _ssd_out(y, z, x, mods, layer, j, first_row, tokens_per_cond, norm_w, w_out):
    t_tokens = x.shape[0]
    tm = TOKEN_TILE
    tok = lambda n: pl.BlockSpec((tm, n), lambda t: (t, 0))
    return pl.pallas_call(
        _ssd_out_kernel,
        grid=(t_tokens // tm,),
        in_specs=[tok(D_INNER), tok(D_INNER), tok(D_MODEL),
                  _mod_spec(layer, tokens_per_cond // tm, first_row),
                  _layer_spec(j, (1, D_INNER)), _layer_spec(j, (D_INNER, D_MODEL))],
        out_specs=tok(D_MODEL),
        out_shape=jax.ShapeDtypeStruct((t_tokens, D_MODEL), F32),
        scratch_shapes=[pltpu.VMEM((D_INNER, D_MODEL), BF16)],
        compiler_params=_params("arbitrary"),
        name="ssd_out",
    )(y, z, x, mods, norm_w, w_out)


def _mlp_kernel(x_ref, mod_ref, nw_ref, w1_ref, w2_ref, fw_ref, o_ref, hb_ref, acc_ref, *, final):
    k = pl.program_id(1)
    last = pl.num_programs(1) - 1

    @pl.when(k == 0)
    def _():
        h = _rms(x_ref[...]) * nw_ref[...]
        hb_ref[...] = (h * (1.0 + mod_ref[4:5, :]) + mod_ref[3:4, :]).astype(BF16)

    a = jnp.maximum(_dot(hb_ref[...], w1_ref[...].astype(BF16)), 0.0)
    part = _dot((a * a).astype(BF16), w2_ref[...].astype(BF16))

    @pl.when(k == 0)
    def _():
        acc_ref[...] = part

    @pl.when(jnp.logical_and(k > 0, k < last))
    def _():
        acc_ref[...] += part

    @pl.when(k == last)
    def _():
        x2 = x_ref[...] + mod_ref[5:6, :] * (acc_ref[...] + part)
        if final:
            x2 = _rms(x2) * fw_ref[...]
        o_ref[...] = x2


def _mlp(x, mods, layer, first_row, tokens_per_cond, norm_w, w1, w2, fw, final):
    t_tokens = x.shape[0]
    tm = MLP_TILE
    tok = pl.BlockSpec((tm, D_MODEL), lambda t, k: (t, 0))
    return pl.pallas_call(
        functools.partial(_mlp_kernel, final=final),
        grid=(t_tokens // tm, D_FF // FF_TILE),
        in_specs=[tok, _mod_spec(layer, tokens_per_cond // tm, first_row),
                  _layer_spec(layer, (1, D_MODEL)),
                  pl.BlockSpec((None, D_MODEL, FF_TILE), lambda t, k: (layer, 0, k)),
                  pl.BlockSpec((None, FF_TILE, D_MODEL), lambda t, k: (layer, k, 0)),
                  _const_spec((1, D_MODEL))],
        out_specs=tok,
        out_shape=jax.ShapeDtypeStruct((t_tokens, D_MODEL), F32),
        scratch_shapes=[pltpu.VMEM((tm, D_MODEL), BF16), pltpu.VMEM((tm, D_MODEL), F32)],
        compiler_params=_params("parallel", "arbitrary"),
        name="mlp",
    )(x, mods, norm_w, w1, w2, fw)


@functools.lru_cache(maxsize=None)
def _dft_matrices(n):
    k = np.arange(n)
    ang = 2.0 * np.pi * ((k[:, None] * k[None, :]) % n) / n
    return np.cos(ang) / np.sqrt(n), np.sin(ang) / np.sqrt(n)


def _fnet_kernel(x_ref, mod_ref, nw_ref, cc_ref, sc_ref, cl_ref, sl_ref, wf_ref, bf_ref, o_ref,
                 f_ref, wbf_ref):
    @pl.when(pl.program_id(0) == 0)
    def _():
        _cast_weight(wbf_ref, wf_ref, D_MODEL)

    x = x_ref[...]
    h = _rms(x) * nw_ref[...]
    hb = (h * (1.0 + mod_ref[1:2, :]) + mod_ref[0:1, :]).astype(BF16)
    for g in range(N_FGROUPS):
        cols = slice(g * D_FG, (g + 1) * D_FG)
        hg = hb[:, cols]
        pc = _dot(hg, cc_ref[...]).astype(BF16)
        ps = _dot(hg, sc_ref[...]).astype(BF16)
        f_ref[:, cols] = (_dot(cl_ref[...], pc) - _dot(sl_ref[...], ps)).astype(BF16)
    o_ref[...] = x + mod_ref[2:3, :] * (_dot(f_ref[...], wbf_ref[...]) + bf_ref[...])


def _fnet(x, mods, layer, j, first_row, seqs_per_cond, seq_len, norm_w, w_f, b_f):
    t_tokens = x.shape[0]
    cc, sc = (jnp.asarray(m, F32).astype(BF16) for m in _dft_matrices(D_FG))
    cl, sl = (jnp.asarray(m, F32).astype(BF16) for m in _dft_matrices(seq_len))
    tok = pl.BlockSpec((seq_len, D_MODEL), lambda t: (t, 0))
    return pl.pallas_call(
        _fnet_kernel,
        grid=(t_tokens // seq_len,),
        in_specs=[tok, _mod_spec(layer, seqs_per_cond, first_row), _layer_spec(layer, (1, D_MODEL)),
                  _const_spec((D_FG, D_FG)), _const_spec((D_FG, D_FG)),
                  _const_spec((seq_len, seq_len)), _const_spec((seq_len, seq_len)),
                  _layer_spec(j, (D_MODEL, D_MODEL)), _layer_spec(j, (1, D_MODEL))],
        out_specs=tok,
        out_shape=jax.ShapeDtypeStruct((t_tokens, D_MODEL), F32),
        scratch_shapes=[pltpu.VMEM((seq_len, D_MODEL), BF16), pltpu.VMEM((D_MODEL, D_MODEL), BF16)],
        compiler_params=_params("arbitrary"),
        name="fnet",
    )(x, mods, norm_w, cc, sc, cl, sl, w_f, b_f)


def _head_perm():
    return np.array([d * N_HEADS + g * HEADS_PER_GROUP + r
                     for g in range(N_GROUPS) for d in range(2) for r in range(HEADS_PER_GROUP)])


def kernel(x_prompt, x_sample, state_ssd, c, c_ctx, ada_w, ada_b, norm_mix_w, norm_mlp_w, ssd_w_in, ssd_conv_w, ssd_conv_b, ssd_dt_bias, ssd_a_log, ssd_d, ssd_norm_w, ssd_w_out, fno_w_out, fno_b_out, mlp_w1, mlp_w2, final_norm_w):
    batch, seq, _ = x_prompt.shape
    dec_batch, dec_seq, _ = x_sample.shape
    n_ssd = ssd_w_in.shape[0]

    cond = jnp.concatenate([c_ctx[None], c, jnp.zeros((N_COND - 1 - dec_batch, D_MODEL), F32)], axis=0)
    mods = _ada_modulation(cond, ada_w, ada_b)

    perm = _head_perm()
    lane_rep = lambda v: jnp.broadcast_to(v.reshape(n_ssd, 2 * N_HEADS)[:, perm][:, :, None],
                                          (n_ssd, 2 * N_HEADS, LANES))
    dtb = lane_rep(ssd_dt_bias)
    alog = lane_rep(ssd_a_log)
    dskip = jnp.repeat(ssd_d, HEAD_DIM, axis=1)[:, None, :]
    wdt = jnp.swapaxes(ssd_w_in[:, :, D_ZX:][:, :, perm], 1, 2).astype(BF16)
    rows3 = lambda v: v[:, None, :]
    norm_mix, norm_mlp = rows3(norm_mix_w), rows3(norm_mlp_w)
    conv_b, ssd_nw, fno_b = rows3(ssd_conv_b), rows3(ssd_norm_w), rows3(fno_b_out)
    final_w = final_norm_w[None, :]

    gw = HEADS_PER_GROUP * HEAD_DIM
    new_state = jax.ShapeDtypeStruct((batch, n_ssd, 2, N_GROUPS, gw, D_STATE), F32)
    h0_all = state_ssd.reshape(dec_batch, n_ssd, 2, N_GROUPS, gw, D_STATE)

    streams = [
        dict(first_row=0, tokens_per_cond=batch * seq, seqs_per_cond=batch, seg=seq, seq_len=seq),
        dict(first_row=1, tokens_per_cond=dec_seq, seqs_per_cond=1, seg=GRID_W, seq_len=dec_seq),
    ]
    acts = [x_prompt.reshape(batch * seq, D_MODEL), x_sample.reshape(dec_batch * dec_seq, D_MODEL)]
    for i in range(DEPTH):
        j = i // 2
        for s, st in enumerate(streams):
            x = acts[s]
            fr, tpc = st["first_row"], st["tokens_per_cond"]
            if i % 2 == 0:
                z, xc, bm, cm, dt = _ssd_in(x, mods, i, j, fr, tpc, st["seg"], norm_mix, ssd_w_in, wdt,
                                            ssd_conv_w, conv_b, dtb)
                if s == 0:
                    y, new_state = _ssd_scan(xc, bm, cm, dt, alog, dskip, st["seq_len"], j, state_out=new_state)
                else:
                    y, _ = _ssd_scan(xc, bm, cm, dt, alog, dskip, st["seq_len"], j, h0=h0_all)
                x = _ssd_out(y, z, x, mods, i, j, fr, tpc, ssd_nw, ssd_w_out)
            else:
                x = _fnet(x, mods, i, j, fr, st["seqs_per_cond"], st["seq_len"], norm_mix, fno_w_out, fno_b)
            acts[s] = _mlp(x, mods, i, fr, tpc, norm_mlp, mlp_w1, mlp_w2, final_w, i == DEPTH - 1)

    y_prompt = acts[0].reshape(batch, seq, D_MODEL)
    y_sample = acts[1].reshape(dec_batch, dec_seq, D_MODEL)
    new_state_ssd = new_state.reshape(batch, n_ssd, 2, N_HEADS, HEAD_DIM, D_STATE)
    return (y_prompt, y_sample, new_state_ssd)
```

```python
import functools

import numpy as np
import jax
import jax.numpy as jnp
from jax import lax
from jax.experimental import pallas as pl
from jax.experimental.pallas import tpu as pltpu

D_MODEL = 1024
DEPTH = 4
GRID_W = 64
D_INNER = 2048
HEAD_DIM = 64
N_HEADS = 32
N_GROUPS = 4
HEADS_PER_GROUP = N_HEADS // N_GROUPS
HG_SHIFT = HEADS_PER_GROUP.bit_length() - 1
D_STATE = 128
CHUNK = 128
D_BC = N_GROUPS * D_STATE
D_XBC = D_INNER + 2 * D_BC
D_ZX = D_INNER + D_XBC
D_IN_PROJ = D_ZX + 2 * N_HEADS
N_FGROUPS = 4
D_FG = D_MODEL // N_FGROUPS
D_FF = 4 * D_MODEL
EPS = 1e-6
N_COND = 8
LOG2E = 1.4426950408889634

LANES = 128
SSD_IN_TILE = 256
SSD_OUT_TILE = 512
SSD_OUT_ROWS = 256
MLP_TILE = 1024
SCAN_SEQS = 1
FNET_SEQS = 2
SHORT_SCAN_CHUNKS = 8
XBC_TILE = 512
FF_TILE = 1024
ADA_TILE = 1536
VMEM_LIMIT = 56 * 1024 * 1024

F32 = jnp.float32
BF16 = jnp.bfloat16


def _dot(a, b):
    return jnp.dot(a, b, preferred_element_type=F32)


def _dot_nt(a, b):
    return lax.dot_general(a, b, (((1,), (1,)), ((), ())), preferred_element_type=F32)


def _silu(x):
    h = 0.5 * x
    return h + h * jnp.tanh(h)


def _rms(x):
    return x * lax.rsqrt(jnp.mean(x * x, axis=-1, keepdims=True) + EPS)


def _const_spec(shape):
    nd = len(shape)
    return pl.BlockSpec(shape, lambda *_: (0,) * nd, pipeline_mode=pl.Buffered(1))


def _layer_spec(layer, tail):
    nd = len(tail)
    return pl.BlockSpec((None,) + tuple(tail), lambda *_: (layer,) + (0,) * nd,
                        pipeline_mode=pl.Buffered(1))


def _mod_spec(layer, tiles_per_cond, first_row):
    return pl.BlockSpec((None, None, 6, D_MODEL),
                        lambda t, *_: (layer, first_row + t // tiles_per_cond, 0, 0))


def _params(*semantics):
    return pltpu.CompilerParams(dimension_semantics=semantics, vmem_limit_bytes=VMEM_LIMIT)


def _cast_weight(dst_ref, src_ref, n_cols, tile=512):
    for j in range(0, n_cols, tile):
        dst_ref[:, j:j + tile] = src_ref[:, j:j + tile].astype(BF16)


def _ada_kernel(c_ref, w_ref, b_ref, o_ref):
    c = c_ref[...]
    s = _silu(c).astype(BF16)
    o_ref[...] = _dot(s, w_ref[...].astype(BF16)) + b_ref[...]


def _ada_modulation(cond, ada_w, ada_b):
    n_out = 6 * D_MODEL
    out = pl.pallas_call(
        _ada_kernel,
        grid=(DEPTH, n_out // ADA_TILE),
        in_specs=[
            pl.BlockSpec((N_COND, D_MODEL), lambda i, n: (0, 0)),
            pl.BlockSpec((None, D_MODEL, ADA_TILE), lambda i, n: (i, 0, n)),
            pl.BlockSpec((None, 1, ADA_TILE), lambda i, n: (i, 0, n)),
        ],
        out_specs=pl.BlockSpec((None, N_COND, ADA_TILE), lambda i, n: (i, 0, n)),
        out_shape=jax.ShapeDtypeStruct((DEPTH, N_COND, n_out), F32),
        compiler_params=_params("parallel", "parallel"),
        name="ada_mod",
    )(cond, ada_w, ada_b.reshape(DEPTH, 1, n_out))
    return out.reshape(DEPTH, N_COND, 6, D_MODEL)


def _ssd_in_kernel(x_ref, mod_ref, nw_ref, win_ref, cw_ref, cb_ref, dtb_ref,
                   z_ref, xs_ref, b_ref, c_ref, dt_ref, wbf_ref, wdt_ref, *, seg):
    @pl.when(pl.program_id(0) == 0)
    def _():
        for r0 in range(0, D_ZX, XBC_TILE):
            wbf_ref[r0:r0 + XBC_TILE, :] = win_ref[r0:r0 + XBC_TILE, :].astype(BF16)
        n = lax.broadcasted_iota(jnp.int32, (2 * N_HEADS, 2 * N_HEADS), 0)
        k = lax.broadcasted_iota(jnp.int32, (2 * N_HEADS, 2 * N_HEADS), 1)
        src = (((n >> HG_SHIFT) & 1) * N_HEADS + (n >> (HG_SHIFT + 1)) * HEADS_PER_GROUP
               + (n & (HEADS_PER_GROUP - 1)))
        pick = jnp.where(k == src, 1.0, 0.0).astype(BF16)
        wdt_ref[...] = _dot(pick, win_ref[D_ZX:D_IN_PROJ, :].astype(BF16)).astype(BF16)

    tm = x_ref.shape[0]
    h = _rms(x_ref[...]) * nw_ref[...]
    h = h * (1.0 + mod_ref[1:2, :]) + mod_ref[0:1, :]
    hb = h.astype(BF16)
    for j in range(D_INNER // XBC_TILE):
        cols = slice(j * XBC_TILE, (j + 1) * XBC_TILE)
        z_ref[:, cols] = _dot_nt(hb, wbf_ref[cols, :]).astype(BF16)

    row = lax.broadcasted_iota(jnp.int32, (tm, XBC_TILE), 0) & (seg - 1)
    first = row == 0
    last = row == seg - 1
    for j in range(D_XBC // XBC_TILE):
        cols = slice(j * XBC_TILE, (j + 1) * XBC_TILE)
        acc = _dot_nt(hb, wbf_ref[D_INNER + j * XBC_TILE:D_INNER + (j + 1) * XBC_TILE, :])
        up = jnp.where(first, 0.0, pltpu.roll(acc, 1, axis=0))
        dn = jnp.where(last, 0.0, pltpu.roll(acc, tm - 1, axis=0))
        o = up * cw_ref[0:1, cols] + acc * cw_ref[1:2, cols] + dn * cw_ref[2:3, cols] + cb_ref[:, cols]
        o = _silu(o).astype(BF16)
        if j < D_INNER // XBC_TILE:
            xs_ref[:, cols] = o
        elif j == D_INNER // XBC_TILE:
            b_ref[...] = o
        else:
            c_ref[...] = o

    dtT = _dot_nt(wdt_ref[...], hb)
    for k in range(tm // CHUNK):
        v = dtT[:, k * CHUNK:(k + 1) * CHUNK] + dtb_ref[...]
        dt_ref[k] = jnp.maximum(v, 0.0) + jnp.log1p(jnp.exp(-jnp.abs(v)))


def _ssd_in(x, mods, layer, j, first_row, tokens_per_cond, seg, norm_w, w_in, conv_w, conv_b, dtb):
    t_tokens = x.shape[0]
    tm = SSD_IN_TILE
    tok = lambda n: pl.BlockSpec((tm, n), lambda t: (t, 0))
    return pl.pallas_call(
        functools.partial(_ssd_in_kernel, seg=seg),
        grid=(t_tokens // tm,),
        in_specs=[
            tok(D_MODEL),
            _mod_spec(layer, tokens_per_cond // tm, first_row),
            _layer_spec(layer, (1, D_MODEL)),
            _layer_spec(j, (D_IN_PROJ, D_MODEL)),
            _layer_spec(j, (3, D_XBC)),
            _layer_spec(j, (1, D_XBC)),
            _layer_spec(j, (2 * N_HEADS, LANES)),
        ],
        out_specs=[
            tok(D_INNER), tok(D_INNER), tok(D_BC), tok(D_BC),
            pl.BlockSpec((tm // CHUNK, 2 * N_HEADS, LANES), lambda t: (t, 0, 0)),
        ],
        out_shape=[
            jax.ShapeDtypeStruct((t_tokens, D_INNER), BF16),
            jax.ShapeDtypeStruct((t_tokens, D_INNER), BF16),
            jax.ShapeDtypeStruct((t_tokens, D_BC), BF16),
            jax.ShapeDtypeStruct((t_tokens, D_BC), BF16),
            jax.ShapeDtypeStruct((t_tokens // CHUNK, 2 * N_HEADS, LANES), F32),
        ],
        scratch_shapes=[pltpu.VMEM((D_ZX, D_MODEL), BF16), pltpu.VMEM((2 * N_HEADS, D_MODEL), BF16)],
        compiler_params=_params("arbitrary"),
        name="ssd_in",
    )(x, mods, norm_w, w_in, conv_w, conv_b, dtb)


def _split3(a):
    hi = a.astype(BF16)
    r = a - hi.astype(F32)
    mid = r.astype(BF16)
    lo = (r - mid.astype(F32)).astype(BF16)
    return hi, mid, lo


def _dot3(a, m):
    hi, mid, lo = _split3(a)
    return _dot(hi, m) + _dot(mid, m) + _dot(lo, m)


def _ssd_scan_kernel(*refs, seq_len, seqs, has_h0, emit_state, has_alias):
    xs_ref, b_ref, c_ref, dt_ref, alog_ref, dskip_ref = refs[:6]
    pos = 6
    h0_ref = refs[pos] if has_h0 else None
    pos += int(has_h0) + int(has_alias)
    y_ref = refs[pos]
    hf_ref = refs[pos + 1] if emit_state else None
    ht_ref = refs[-1]

    nc = seq_len // CHUNK
    hg = HEADS_PER_GROUP
    rows = lax.broadcasted_iota(jnp.int32, (CHUNK, CHUNK), 0)
    lanes = lax.broadcasted_iota(jnp.int32, (CHUNK, CHUNK), 1)
    tril = rows >= lanes
    triu = rows <= lanes
    tril_b = jnp.where(tril, 1.0, 0.0).astype(BF16)
    triu_b = jnp.where(triu, 1.0, 0.0).astype(BF16)
    ones_b = jnp.ones((CHUNK, CHUNK), BF16)
    low_half = lanes < HEAD_DIM
    low_b = jnp.where(low_half, 1.0, 0.0).astype(BF16)
    high_b = jnp.where(low_half, 0.0, 1.0).astype(BF16)
    a2 = -jnp.exp(alog_ref[...]) * LOG2E

    sum_tot_b = [jnp.concatenate([triu_b, ones_b], axis=1), jnp.concatenate([tril_b, ones_b], axis=1)]

    def scan_short(s):
        below = rows > lanes
        above = rows < lanes
        y, st, cols, cdp, c16 = {}, {}, {}, {}, {}
        for c in range(nc):
            t0 = s * seq_len + c * CHUNK
            dt16 = dt_ref[s * nc + c]
            da16 = dt16 * a2
            parts = jnp.concatenate(_split3(da16), axis=0)
            dts, das, cs_t, w_t, cd, src_t = [], [], [], [], [], []
            for d in range(2):
                res = _dot(parts, sum_tot_b[d])
                sums = (res[0:2 * hg] + res[2 * hg:4 * hg] + res[4 * hg:6 * hg])[d * hg:(d + 1) * hg]
                dts.append(dt16[d * hg:(d + 1) * hg])
                das.append(da16[d * hg:(d + 1) * hg])
                cs_t.append(sums[:, 0:CHUNK])
                tot = sums[:, CHUNK:2 * CHUNK]
                w_t.append(dts[d] * jnp.exp2(tot - cs_t[d]))
                cd.append(jnp.exp2(tot))
                src_t.append(cs_t[d] - jnp.log(dts[d]) * LOG2E)
            diag_t = jnp.log(dts[0] + dts[1]) * LOG2E
            bc = b_ref[t0:t0 + CHUNK, :]
            c16[c] = c_ref[t0:t0 + CHUNK, :]
            cb = _dot_nt(c16[c], bc)
            b_t = bc.astype(F32).T
            for q in range(hg // 2):
                sl = slice(q * LANES, (q + 1) * LANES)
                xpb = xs_ref[t0:t0 + CHUNK, sl]
                x2 = jnp.concatenate([xpb * low_b, xpb * high_b], axis=0)
                m_parts, bw_parts, col_parts = [], ([], []), ([], [])
                for r in (2 * q, 2 * q + 1):
                    col_f = jnp.sum(jnp.where(tril, das[0][r:r + 1, :], 0.0), axis=1, keepdims=True)
                    col_b = jnp.sum(jnp.where(triu, das[1][r:r + 1, :], 0.0), axis=1, keepdims=True)
                    arg = jnp.where(below, col_f - src_t[0][r:r + 1, :],
                                    jnp.where(above, col_b - src_t[1][r:r + 1, :], diag_t[r:r + 1, :]))
                    m_parts.append((cb * jnp.exp2(arg)).astype(BF16))
                    for d, col in ((0, col_f), (1, col_b)):
                        bw_parts[d].append((b_t * w_t[d][r:r + 1, :]).astype(BF16))
                        col_parts[d].append(col)
                y[c, q] = _dot(jnp.concatenate(m_parts, axis=1), x2) + xpb.astype(F32) * dskip_ref[:, sl]
                for d in range(2):
                    st[d, c, q] = _dot(jnp.concatenate(bw_parts[d], axis=1), x2)
                    cols[d, c, q] = col_parts[d]
                    cdp[d, c, q] = jnp.where(low_half, cd[d][2 * q:2 * q + 1, :], cd[d][2 * q + 1:2 * q + 2, :])

        for d in range(2):
            order = range(nc) if d == 0 else range(nc - 1, -1, -1)
            for q in range(hg // 2):
                sl = slice(q * LANES, (q + 1) * LANES)
                h = h0_ref[s, d, sl, :].T if has_h0 else None
                for c in order:
                    if h is None:
                        h = st[d, c, q]
                    else:
                        ecs = jnp.exp2(jnp.where(low_half, cols[d, c, q][0], cols[d, c, q][1]))
                        y[c, q] = y[c, q] + _dot(c16[c], h.astype(BF16)) * ecs
                        h = h * cdp[d, c, q] + st[d, c, q]
                if emit_state:
                    hf_ref[s, d, sl, :] = h.T
        for c in range(nc):
            for q in range(hg // 2):
                t0 = s * seq_len + c * CHUNK
                y_ref[t0:t0 + CHUNK, q * LANES:(q + 1) * LANES] = y[c, q].astype(BF16)

    if nc <= SHORT_SCAN_CHUNKS:
        for s in range(seqs):
            scan_short(s)
        return

    y_ref[...] = jnp.zeros(y_ref.shape, y_ref.dtype)
    for s in range(seqs):
        for d in range(2):
            for q in range(hg // 2):
                sl = slice(q * LANES, (q + 1) * LANES)
                if has_h0:
                    ht_ref[s, d, :, sl] = h0_ref[s, d, sl, :].T
                else:
                    ht_ref[s, d, :, sl] = jnp.zeros((D_STATE, LANES), F32)

    def chunk_step(s, d, c):
        mask = tril if d == 0 else triu
        t0 = s * seq_len + pl.multiple_of(c * CHUNK, CHUNK)
        dt16 = dt_ref[s * nc + c]
        da16 = dt16 * a2
        res = _dot(jnp.concatenate(_split3(da16), axis=0), sum_tot_b[d])
        sums = res[0:2 * hg] + res[2 * hg:4 * hg] + res[4 * hg:6 * hg]
        cs_t = sums[d * hg:(d + 1) * hg, 0:CHUNK]
        tot = sums[d * hg:(d + 1) * hg, CHUNK:2 * CHUNK]
        dt8 = dt16[d * hg:(d + 1) * hg]
        da8 = da16[d * hg:(d + 1) * hg]
        w_t = dt8 * jnp.exp2(tot - cs_t)
        cd = jnp.exp2(tot)
        src_t = cs_t - jnp.log(dt8) * LOG2E
        bc = b_ref[pl.ds(t0, CHUNK), :]
        cb16 = c_ref[pl.ds(t0, CHUNK), :]
        cb = _dot_nt(cb16, bc)
        b_t = bc.astype(F32).T

        for q in range(hg // 2):
            sl = slice(q * LANES, (q + 1) * LANES)
            xpb = xs_ref[pl.ds(t0, CHUNK), sl]
            x2 = jnp.concatenate([xpb * low_b, xpb * high_b], axis=0)
            m_parts, bw_parts, c_parts = [], [], []
            for r in (2 * q, 2 * q + 1):
                cs_col = jnp.sum(jnp.where(mask, da8[r:r + 1, :], 0.0), axis=1, keepdims=True)
                decay = jnp.where(mask, jnp.exp2(cs_col - src_t[r:r + 1, :]), 0.0)
                m_parts.append((cb * decay).astype(BF16))
                bw_parts.append((b_t * w_t[r:r + 1, :]).astype(BF16))
                c_parts.append(cs_col)
            y = _dot(jnp.concatenate(m_parts, axis=1), x2) + y_ref[pl.ds(t0, CHUNK), sl].astype(F32)
            st = _dot(jnp.concatenate(bw_parts, axis=1), x2)
            htp = ht_ref[s, d, :, sl]
            ecs = jnp.exp2(jnp.where(low_half, c_parts[0], c_parts[1]))
            cdp = jnp.where(low_half, cd[2 * q:2 * q + 1, :], cd[2 * q + 1:2 * q + 2, :])
            y = y + _dot(cb16, htp.astype(BF16)) * ecs
            ht_ref[s, d, :, sl] = htp * cdp + st
            if d == 0:
                y = y + xpb.astype(F32) * dskip_ref[:, sl]
            y_ref[pl.ds(t0, CHUNK), sl] = y.astype(BF16)

    def body(i, carry):
        for s in range(seqs):
            chunk_step(s, 0, i)
            chunk_step(s, 1, nc - 1 - i)
        return carry

    lax.fori_loop(0, nc, body, 0, unroll=2)

    if emit_state:
        for s in range(seqs):
            for d in range(2):
                for q in range(hg // 2):
                    sl = slice(q * LANES, (q + 1) * LANES)
                    hf_ref[s, d, sl, :] = ht_ref[s, d, :, sl].T


def _ssd_scan(xs, bm, cm, dt, alog, dskip, seq_len, seqs, ssd_index, h0=None, state_out=None):
    t_tokens = xs.shape[0]
    rows = seq_len * seqs
    nb = t_tokens // rows
    gw = HEADS_PER_GROUP * HEAD_DIM
    nc = seq_len // CHUNK
    in_specs = [
        pl.BlockSpec((rows, gw), lambda b, g: (b, g)),
        pl.BlockSpec((rows, D_STATE), lambda b, g: (b, g)),
        pl.BlockSpec((rows, D_STATE), lambda b, g: (b, g)),
        pl.BlockSpec((seqs * nc, 2 * HEADS_PER_GROUP, LANES), lambda b, g: (b, g, 0)),
        pl.BlockSpec((None, 2 * HEADS_PER_GROUP, LANES), lambda b, g: (ssd_index, g, 0)),
        pl.BlockSpec((None, 1, gw), lambda b, g: (ssd_index, 0, g)),
    ]
    args = [xs, bm, cm, dt, alog, dskip]
    has_h0 = h0 is not None
    if has_h0:
        in_specs.append(pl.BlockSpec((seqs, None, 2, None, gw, D_STATE),
                                     lambda b, g: (b, ssd_index, 0, g, 0, 0)))
        args.append(h0)
    emit_state = state_out is not None
    has_alias = emit_state and not isinstance(state_out, jax.ShapeDtypeStruct)
    out_specs = [pl.BlockSpec((rows, gw), lambda b, g: (b, g))]
    out_shape = [jax.ShapeDtypeStruct((t_tokens, D_INNER), BF16)]
    aliases = {}
    if emit_state:
        if has_alias:
            in_specs.append(pl.BlockSpec(memory_space=pl.ANY))
            args.append(state_out)
            aliases = {len(args) - 1: 1}
        state_sds = jax.ShapeDtypeStruct(state_out.shape, state_out.dtype)
        out_specs.append(pl.BlockSpec((seqs, None, 2, None, gw, D_STATE),
                                      lambda b, g: (b, ssd_index, 0, g, 0, 0)))
        out_shape.append(state_sds)
    res = pl.pallas_call(
        functools.partial(_ssd_scan_kernel, seq_len=seq_len, seqs=seqs, has_h0=has_h0,
                          emit_state=emit_state, has_alias=has_alias),
        grid=(nb, N_GROUPS),
        in_specs=in_specs,
        out_specs=out_specs,
        out_shape=out_shape,
        scratch_shapes=[pltpu.VMEM((seqs, 2, D_STATE, gw), F32)],
        input_output_aliases=aliases,
        compiler_params=_params("parallel", "parallel"),
        name="ssd_scan",
    )(*args)
    return res if emit_state else (res[0], None)


def _mlp_input(x, mod_ref, nw_ref):
    h = _rms(x) * nw_ref[...]
    return (h * (1.0 + mod_ref[4:5, :]) + mod_ref[3:4, :]).astype(BF16)


def _ssd_out_kernel(y_ref, z_ref, x_ref, mod_ref, nw_ref, wo_ref, nwm_ref, o_ref, hb_ref, wbf_ref):
    @pl.when(pl.program_id(0) == 0)
    def _():
        nw_cols = jnp.concatenate([nw_ref[...]] * (XBC_TILE // LANES), axis=1)
        for j in range(0, D_MODEL, XBC_TILE):
            wbf_ref[:, j:j + XBC_TILE] = (wo_ref[:, j:j + XBC_TILE] * nw_cols).astype(BF16)

    for r0 in range(0, x_ref.shape[0], SSD_OUT_ROWS):
        rs = slice(r0, r0 + SSD_OUT_ROWS)
        g = y_ref[rs, :] * _silu(z_ref[rs, :])
        gf = g.astype(F32)
        scale = lax.rsqrt(jnp.mean(gf * gf, axis=-1, keepdims=True) + EPS)
        x1 = x_ref[rs, :] + mod_ref[2:3, :] * (_dot(g, wbf_ref[...]) * scale)
        o_ref[rs, :] = x1
        hb_ref[rs, :] = _mlp_input(x1, mod_ref, nwm_ref)


def _ssd_out(y, z, x, mods, layer, j, first_row, tokens_per_cond, norm_w, w_out, norm_mlp):
    t_tokens = x.shape[0]
    tm = SSD_OUT_TILE
    tok = lambda n: pl.BlockSpec((tm, n), lambda t: (t, 0))
    return pl.pallas_call(
        _ssd_out_kernel,
        grid=(t_tokens // tm,),
        in_specs=[tok(D_INNER), tok(D_INNER), tok(D_MODEL),
                  _mod_spec(layer, tokens_per_cond // tm, first_row),
                  _layer_spec(j, (D_INNER, LANES)), _layer_spec(j, (D_INNER, D_MODEL)),
                  _layer_spec(layer, (1, D_MODEL))],
        out_specs=[tok(D_MODEL), tok(D_MODEL)],
        out_shape=[jax.ShapeDtypeStruct((t_tokens, D_MODEL), F32),
                   jax.ShapeDtypeStruct((t_tokens, D_MODEL), BF16)],
        scratch_shapes=[pltpu.VMEM((D_INNER, D_MODEL), BF16)],
        compiler_params=_params("arbitrary"),
        name="ssd_out",
    )(y, z, x, mods, norm_w, w_out, norm_mlp)


def _mlp_kernel(x_ref, hb_ref, mod_ref, w1_ref, w2_ref, fw_ref, o_ref, acc_ref, *, final):
    k = pl.program_id(1)
    last = pl.num_programs(1) - 1

    def block():
        a = jnp.maximum(_dot(hb_ref[...], w1_ref[...].astype(BF16)), 0.0)
        return _dot((a * a).astype(BF16), w2_ref[...].astype(BF16))

    @pl.when(k == 0)
    def _():
        acc_ref[...] = block()

    @pl.when(jnp.logical_and(k > 0, k < last))
    def _():
        acc_ref[...] += block()

    @pl.when(k == last)
    def _():
        x2 = x_ref[...] + mod_ref[5:6, :] * (acc_ref[...] + block())
        if final:
            x2 = _rms(x2) * fw_ref[...]
        o_ref[...] = x2


def _mlp(x, hb, mods, layer, first_row, tokens_per_cond, w1, w2, fw, final):
    t_tokens = x.shape[0]
    tm = MLP_TILE
    assert D_FF // FF_TILE >= 2
    tok = pl.BlockSpec((tm, D_MODEL), lambda t, k: (t, 0))
    return pl.pallas_call(
        functools.partial(_mlp_kernel, final=final),
        grid=(t_tokens // tm, D_FF // FF_TILE),
        in_specs=[tok, tok, _mod_spec(layer, tokens_per_cond // tm, first_row),
                  pl.BlockSpec((None, D_MODEL, FF_TILE), lambda t, k: (layer, 0, k)),
                  pl.BlockSpec((None, FF_TILE, D_MODEL), lambda t, k: (layer, k, 0)),
                  _const_spec((1, D_MODEL))],
        out_specs=tok,
        out_shape=jax.ShapeDtypeStruct((t_tokens, D_MODEL), F32),
        scratch_shapes=[pltpu.VMEM((tm, D_MODEL), F32)],
        compiler_params=_params("parallel", "arbitrary"),
        name="mlp",
    )(x, hb, mods, w1, w2, fw)


@functools.lru_cache(maxsize=None)
def _dft_matrices(n):
    k = np.arange(n)
    ang = 2.0 * np.pi * ((k[:, None] * k[None, :]) % n) / n
    return np.cos(ang) / np.sqrt(n), np.sin(ang) / np.sqrt(n)


def _fnet_kernel(x_ref, mod_ref, nw_ref, chan_ref, seqm_ref, wf_ref, bf_ref, nwm_ref, o_ref, hb_ref,
                 f_ref, wbf_ref, *, seq_len):
    @pl.when(pl.program_id(0) == 0)
    def _():
        _cast_weight(wbf_ref, wf_ref, D_MODEL)

    x = x_ref[...]
    h = _rms(x) * nw_ref[...]
    hb = (h * (1.0 + mod_ref[1:2, :]) + mod_ref[0:1, :]).astype(BF16)
    for g in range(N_FGROUPS):
        cols = slice(g * D_FG, (g + 1) * D_FG)
        p = _dot(hb[:, cols], chan_ref[...]).astype(BF16)
        for r0 in range(0, x_ref.shape[0], seq_len):
            rs = slice(r0, r0 + seq_len)
            stacked = jnp.concatenate([p[rs, :D_FG], p[rs, D_FG:]], axis=0)
            f_ref[rs, cols] = _dot(seqm_ref[...], stacked).astype(BF16)
    x1 = x + mod_ref[2:3, :] * (_dot(f_ref[...], wbf_ref[...]) + bf_ref[...])
    o_ref[...] = x1
    hb_ref[...] = _mlp_input(x1, mod_ref, nwm_ref)


def _fnet(x, mods, layer, j, first_row, seqs_per_cond, seq_len, seqs, norm_w, w_f, b_f, norm_mlp):
    t_tokens = x.shape[0]
    rows = seq_len * seqs
    cc, sc = _dft_matrices(D_FG)
    cl, sl = _dft_matrices(seq_len)
    chan = jnp.asarray(np.concatenate([cc, sc], axis=1), F32).astype(BF16)
    seqm = jnp.asarray(np.concatenate([cl, -sl], axis=1), F32).astype(BF16)
    tok = pl.BlockSpec((rows, D_MODEL), lambda t: (t, 0))
    return pl.pallas_call(
        functools.partial(_fnet_kernel, seq_len=seq_len),
        grid=(t_tokens // rows,),
        in_specs=[tok, _mod_spec(layer, seqs_per_cond // seqs, first_row), _layer_spec(layer, (1, D_MODEL)),
                  _const_spec((D_FG, 2 * D_FG)), _const_spec((seq_len, 2 * seq_len)),
                  _layer_spec(j, (D_MODEL, D_MODEL)), _layer_spec(j, (1, D_MODEL)),
                  _layer_spec(layer, (1, D_MODEL))],
        out_specs=[tok, tok],
        out_shape=[jax.ShapeDtypeStruct((t_tokens, D_MODEL), F32),
                   jax.ShapeDtypeStruct((t_tokens, D_MODEL), BF16)],
        scratch_shapes=[pltpu.VMEM((rows, D_MODEL), BF16), pltpu.VMEM((D_MODEL, D_MODEL), BF16)],
        compiler_params=_params("arbitrary"),
        name="fnet",
    )(x, mods, norm_w, chan, seqm, w_f, b_f, norm_mlp)


def _head_perm():
    return np.array([d * N_HEADS + g * HEADS_PER_GROUP + r
                     for g in range(N_GROUPS) for d in range(2) for r in range(HEADS_PER_GROUP)])


def kernel(x_prompt, x_sample, state_ssd, c, c_ctx, ada_w, ada_b, norm_mix_w, norm_mlp_w, ssd_w_in, ssd_conv_w, ssd_conv_b, ssd_dt_bias, ssd_a_log, ssd_d, ssd_norm_w, ssd_w_out, fno_w_out, fno_b_out, mlp_w1, mlp_w2, final_norm_w):
    batch, seq, _ = x_prompt.shape
    dec_batch, dec_seq, _ = x_sample.shape
    n_ssd = ssd_w_in.shape[0]

    cond = jnp.concatenate([c_ctx[None], c, jnp.zeros((N_COND - 1 - dec_batch, D_MODEL), F32)], axis=0)
    mods = _ada_modulation(cond, ada_w, ada_b)

    perm = _head_perm()
    lane_rep = lambda v: jnp.broadcast_to(v.reshape(n_ssd, 2 * N_HEADS)[:, perm][:, :, None],
                                          (n_ssd, 2 * N_HEADS, LANES))
    dtb = lane_rep(ssd_dt_bias)
    alog = lane_rep(ssd_a_log)
    dskip = jnp.repeat(ssd_d, HEAD_DIM, axis=1)[:, None, :]
    rows3 = lambda v: v[:, None, :]
    w_in_t = jnp.swapaxes(ssd_w_in, 1, 2)
    norm_mix, norm_mlp = rows3(norm_mix_w), rows3(norm_mlp_w)
    conv_b, fno_b = rows3(ssd_conv_b), rows3(fno_b_out)
    ssd_nw = jnp.broadcast_to(ssd_norm_w[:, :, None], (n_ssd, D_INNER, LANES))
    final_w = final_norm_w[None, :]

    gw = HEADS_PER_GROUP * HEAD_DIM
    new_state = jax.ShapeDtypeStruct((batch, n_ssd, 2, N_GROUPS, gw, D_STATE), F32)
    h0_all = state_ssd.reshape(dec_batch, n_ssd, 2, N_GROUPS, gw, D_STATE)

    streams = [
        dict(first_row=0, tokens_per_cond=batch * seq, seqs_per_cond=batch, seg=seq, seq_len=seq,
             scan_seqs=SCAN_SEQS if batch % SCAN_SEQS == 0 else 1,
             fnet_seqs=FNET_SEQS if batch % FNET_SEQS == 0 else 1),
        dict(first_row=1, tokens_per_cond=dec_seq, seqs_per_cond=1, seg=GRID_W, seq_len=dec_seq, scan_seqs=1,
             fnet_seqs=1),
    ]
    acts = [x_prompt.reshape(batch * seq, D_MODEL), x_sample.reshape(dec_batch * dec_seq, D_MODEL)]
    for i in range(DEPTH):
        j = i // 2
        for s, st in enumerate(streams):
            x = acts[s]
            fr, tpc = st["first_row"], st["tokens_per_cond"]
            if i % 2 == 0:
                z, xc, bm, cm, dt = _ssd_in(x, mods, i, j, fr, tpc, st["seg"], norm_mix, w_in_t,
                                            ssd_conv_w, conv_b, dtb)
                if s == 0:
                    y, new_state = _ssd_scan(xc, bm, cm, dt, alog, dskip, st["seq_len"], st["scan_seqs"], j,
                                             state_out=new_state)
                else:
                    y, _ = _ssd_scan(xc, bm, cm, dt, alog, dskip, st["seq_len"], st["scan_seqs"], j, h0=h0_all)
                x, hb = _ssd_out(y, z, x, mods, i, j, fr, tpc, ssd_nw, ssd_w_out, norm_mlp)
            else:
                x, hb = _fnet(x, mods, i, j, fr, st["seqs_per_cond"], st["seq_len"], st["fnet_seqs"], norm_mix,
                              fno_w_out, fno_b, norm_mlp)
            acts[s] = _mlp(x, hb, mods, i, fr, tpc, mlp_w1, mlp_w2, final_w, i == DEPTH - 1)

    y_prompt = acts[0].reshape(batch, seq, D_MODEL)
    y_sample = acts[1].reshape(dec_batch, dec_seq, D_MODEL)
    new_state_ssd = new_state.reshape(batch, n_ssd, 2, N_HEADS, HEAD_DIM, D_STATE)
    return (y_prompt, y_sample, new_state_ssd)
```

```python
import functools

import numpy as np
import jax
import jax.numpy as jnp
from jax import lax
from jax.experimental import pallas as pl
from jax.experimental.pallas import tpu as pltpu

D_MODEL = 1024
DEPTH = 4
GRID_W = 64
D_INNER = 2048
HEAD_DIM = 64
N_HEADS = 32
N_GROUPS = 4
HEADS_PER_GROUP = N_HEADS // N_GROUPS
HG_SHIFT = HEADS_PER_GROUP.bit_length() - 1
D_STATE = 128
CHUNK = 128
D_BC = N_GROUPS * D_STATE
D_XBC = D_INNER + 2 * D_BC
D_ZX = D_INNER + D_XBC
D_IN_PROJ = D_ZX + 2 * N_HEADS
N_FGROUPS = 4
D_FG = D_MODEL // N_FGROUPS
D_FF = 4 * D_MODEL
EPS = 1e-6
N_COND = 8
LOG2E = 1.4426950408889634

LANES = 128
SSD_IN_TILE = 256
SSD_OUT_TILE = 512
SSD_OUT_ROWS = 256
MLP_TILE = 1024
SCAN_SEQS = 4
FNET_SEQS = 4
SHORT_SCAN_CHUNKS = 8
XBC_TILE = 512
FF_TILE = 1024
ADA_TILE = 1536
VMEM_LIMIT = 56 * 1024 * 1024

F32 = jnp.float32
BF16 = jnp.bfloat16


def _dot(a, b):
    return jnp.dot(a, b, preferred_element_type=F32)


def _dot_nt(a, b):
    return lax.dot_general(a, b, (((1,), (1,)), ((), ())), preferred_element_type=F32)


def _silu(x):
    h = 0.5 * x
    return h + h * jnp.tanh(h)


def _rms(x):
    return x * lax.rsqrt(jnp.mean(x * x, axis=-1, keepdims=True) + EPS)


def _const_spec(shape):
    nd = len(shape)
    return pl.BlockSpec(shape, lambda *_: (0,) * nd, pipeline_mode=pl.Buffered(1))


def _layer_spec(layer, tail):
    nd = len(tail)
    return pl.BlockSpec((None,) + tuple(tail), lambda *_: (layer,) + (0,) * nd,
                        pipeline_mode=pl.Buffered(1))


def _mod_spec(layer, tiles_per_cond, first_row):
    return pl.BlockSpec((None, None, 6, D_MODEL),
                        lambda t, *_: (layer, first_row + t // tiles_per_cond, 0, 0))


def _params(*semantics):
    return pltpu.CompilerParams(dimension_semantics=semantics, vmem_limit_bytes=VMEM_LIMIT)


def _cast_weight(dst_ref, src_ref, n_cols, tile=512):
    for j in range(0, n_cols, tile):
        dst_ref[:, j:j + tile] = src_ref[:, j:j + tile].astype(BF16)


def _ada_kernel(c_ref, w_ref, b_ref, o_ref):
    c = c_ref[...]
    s = _silu(c).astype(BF16)
    o_ref[...] = _dot(s, w_ref[...].astype(BF16)) + b_ref[...]


def _ada_modulation(cond, ada_w, ada_b):
    n_out = 6 * D_MODEL
    out = pl.pallas_call(
        _ada_kernel,
        grid=(DEPTH, n_out // ADA_TILE),
        in_specs=[
            pl.BlockSpec((N_COND, D_MODEL), lambda i, n: (0, 0)),
            pl.BlockSpec((None, D_MODEL, ADA_TILE), lambda i, n: (i, 0, n)),
            pl.BlockSpec((None, 1, ADA_TILE), lambda i, n: (i, 0, n)),
        ],
        out_specs=pl.BlockSpec((None, N_COND, ADA_TILE), lambda i, n: (i, 0, n)),
        out_shape=jax.ShapeDtypeStruct((DEPTH, N_COND, n_out), F32),
        compiler_params=_params("parallel", "parallel"),
        name="ada_mod",
    )(cond, ada_w, ada_b.reshape(DEPTH, 1, n_out))
    return out.reshape(DEPTH, N_COND, 6, D_MODEL)


def _ssd_in_kernel(x_ref, mod_ref, nw_ref, win_ref, cw_ref, cb_ref, dtb_ref,
                   z_ref, xs_ref, b_ref, c_ref, dt_ref, wbf_ref, wdt_ref, *, seg):
    @pl.when(pl.program_id(0) == 0)
    def _():
        for r0 in range(0, D_ZX, XBC_TILE):
            wbf_ref[r0:r0 + XBC_TILE, :] = win_ref[r0:r0 + XBC_TILE, :].astype(BF16)
        n = lax.broadcasted_iota(jnp.int32, (2 * N_HEADS, 2 * N_HEADS), 0)
        k = lax.broadcasted_iota(jnp.int32, (2 * N_HEADS, 2 * N_HEADS), 1)
        src = (((n >> HG_SHIFT) & 1) * N_HEADS + (n >> (HG_SHIFT + 1)) * HEADS_PER_GROUP
               + (n & (HEADS_PER_GROUP - 1)))
        pick = jnp.where(k == src, 1.0, 0.0).astype(BF16)
        wdt_ref[...] = _dot(pick, win_ref[D_ZX:D_IN_PROJ, :].astype(BF16)).astype(BF16)

    tm = x_ref.shape[0]
    h = _rms(x_ref[...]) * nw_ref[...]
    h = h * (1.0 + mod_ref[1:2, :]) + mod_ref[0:1, :]
    hb = h.astype(BF16)
    for j in range(D_INNER // XBC_TILE):
        cols = slice(j * XBC_TILE, (j + 1) * XBC_TILE)
        z_ref[:, cols] = _dot_nt(hb, wbf_ref[cols, :]).astype(BF16)

    row = lax.broadcasted_iota(jnp.int32, (tm, XBC_TILE), 0) & (seg - 1)
    first = row == 0
    last = row == seg - 1
    for j in range(D_XBC // XBC_TILE):
        cols = slice(j * XBC_TILE, (j + 1) * XBC_TILE)
        acc = _dot_nt(hb, wbf_ref[D_INNER + j * XBC_TILE:D_INNER + (j + 1) * XBC_TILE, :])
        up = jnp.where(first, 0.0, pltpu.roll(acc, 1, axis=0))
        dn = jnp.where(last, 0.0, pltpu.roll(acc, tm - 1, axis=0))
        o = up * cw_ref[0:1, cols] + acc * cw_ref[1:2, cols] + dn * cw_ref[2:3, cols] + cb_ref[:, cols]
        o = _silu(o).astype(BF16)
        if j < D_INNER // XBC_TILE:
            xs_ref[:, cols] = o
        elif j == D_INNER // XBC_TILE:
            b_ref[...] = o
        else:
            c_ref[...] = o

    dtT = _dot_nt(wdt_ref[...], hb)
    for k in range(tm // CHUNK):
        v = dtT[:, k * CHUNK:(k + 1) * CHUNK] + dtb_ref[...]
        dt_ref[k] = jnp.maximum(v, 0.0) + jnp.log1p(jnp.exp(-jnp.abs(v)))


def _ssd_in(x, mods, layer, j, first_row, tokens_per_cond, seg, norm_w, w_in, conv_w, conv_b, dtb):
    t_tokens = x.shape[0]
    tm = SSD_IN_TILE
    tok = lambda n: pl.BlockSpec((tm, n), lambda t: (t, 0))
    return pl.pallas_call(
        functools.partial(_ssd_in_kernel, seg=seg),
        grid=(t_tokens // tm,),
        in_specs=[
            tok(D_MODEL),
            _mod_spec(layer, tokens_per_cond // tm, first_row),
            _layer_spec(layer, (1, D_MODEL)),
            _layer_spec(j, (D_IN_PROJ, D_MODEL)),
            _layer_spec(j, (3, D_XBC)),
            _layer_spec(j, (1, D_XBC)),
            _layer_spec(j, (2 * N_HEADS, LANES)),
        ],
        out_specs=[
            tok(D_INNER), tok(D_INNER), tok(D_BC), tok(D_BC),
            pl.BlockSpec((tm // CHUNK, 2 * N_HEADS, LANES), lambda t: (t, 0, 0)),
        ],
        out_shape=[
            jax.ShapeDtypeStruct((t_tokens, D_INNER), BF16),
            jax.ShapeDtypeStruct((t_tokens, D_INNER), BF16),
            jax.ShapeDtypeStruct((t_tokens, D_BC), BF16),
            jax.ShapeDtypeStruct((t_tokens, D_BC), BF16),
            jax.ShapeDtypeStruct((t_tokens // CHUNK, 2 * N_HEADS, LANES), F32),
        ],
        scratch_shapes=[pltpu.VMEM((D_ZX, D_MODEL), BF16), pltpu.VMEM((2 * N_HEADS, D_MODEL), BF16)],
        compiler_params=_params("arbitrary"),
        name="ssd_in",
    )(x, mods, norm_w, w_in, conv_w, conv_b, dtb)


def _split3(a):
    hi = a.astype(BF16)
    r = a - hi.astype(F32)
    mid = r.astype(BF16)
    lo = (r - mid.astype(F32)).astype(BF16)
    return hi, mid, lo


def _dot3(a, m):
    hi, mid, lo = _split3(a)
    return _dot(hi, m) + _dot(mid, m) + _dot(lo, m)


def _ssd_scan_kernel(*refs, seq_len, seqs, has_h0, emit_state, has_alias):
    xs_ref, b_ref, c_ref, dt_ref, alog_ref, dskip_ref = refs[:6]
    pos = 6
    h0_ref = refs[pos] if has_h0 else None
    pos += int(has_h0) + int(has_alias)
    y_ref = refs[pos]
    hf_ref = refs[pos + 1] if emit_state else None
    ht_ref = refs[-1]

    nc = seq_len // CHUNK
    hg = HEADS_PER_GROUP
    rows = lax.broadcasted_iota(jnp.int32, (CHUNK, CHUNK), 0)
    lanes = lax.broadcasted_iota(jnp.int32, (CHUNK, CHUNK), 1)
    tril = rows >= lanes
    triu = rows <= lanes
    tril_b = jnp.where(tril, 1.0, 0.0).astype(BF16)
    triu_b = jnp.where(triu, 1.0, 0.0).astype(BF16)
    ones_b = jnp.ones((CHUNK, CHUNK), BF16)
    low_half = lanes < HEAD_DIM
    low_b = jnp.where(low_half, 1.0, 0.0).astype(BF16)
    high_b = jnp.where(low_half, 0.0, 1.0).astype(BF16)
    a2 = -jnp.exp(alog_ref[...]) * LOG2E

    sum_tot_b = [jnp.concatenate([triu_b, ones_b], axis=1), jnp.concatenate([tril_b, ones_b], axis=1)]

    def scan_short(s):
        below = rows > lanes
        above = rows < lanes
        y, st, cols, cdp, c16 = {}, {}, {}, {}, {}
        for c in range(nc):
            t0 = s * seq_len + c * CHUNK
            dt16 = dt_ref[s * nc + c]
            da16 = dt16 * a2
            parts = jnp.concatenate(_split3(da16), axis=0)
            dts, das, cs_t, w_t, cd, src_t = [], [], [], [], [], []
            for d in range(2):
                res = _dot(parts, sum_tot_b[d])
                sums = (res[0:2 * hg] + res[2 * hg:4 * hg] + res[4 * hg:6 * hg])[d * hg:(d + 1) * hg]
                dts.append(dt16[d * hg:(d + 1) * hg])
                das.append(da16[d * hg:(d + 1) * hg])
                cs_t.append(sums[:, 0:CHUNK])
                tot = sums[:, CHUNK:2 * CHUNK]
                w_t.append(dts[d] * jnp.exp2(tot - cs_t[d]))
                cd.append(jnp.exp2(tot))
                src_t.append(cs_t[d] - jnp.log(dts[d]) * LOG2E)
            diag_t = jnp.log(dts[0] + dts[1]) * LOG2E
            bc = b_ref[t0:t0 + CHUNK, :]
            c16[c] = c_ref[t0:t0 + CHUNK, :]
            cb = _dot_nt(c16[c], bc)
            b_t = bc.astype(F32).T
            for q in range(hg // 2):
                sl = slice(q * LANES, (q + 1) * LANES)
                xpb = xs_ref[t0:t0 + CHUNK, sl]
                x2 = jnp.concatenate([xpb * low_b, xpb * high_b], axis=0)
                m_parts, bw_parts, col_parts = [], ([], []), ([], [])
                for r in (2 * q, 2 * q + 1):
                    col_f = jnp.sum(jnp.where(tril, das[0][r:r + 1, :], 0.0), axis=1, keepdims=True)
                    col_b = jnp.sum(jnp.where(triu, das[1][r:r + 1, :], 0.0), axis=1, keepdims=True)
                    arg = jnp.where(below, col_f - src_t[0][r:r + 1, :],
                                    jnp.where(above, col_b - src_t[1][r:r + 1, :], diag_t[r:r + 1, :]))
                    m_parts.append((cb * jnp.exp2(arg)).astype(BF16))
                    for d, col in ((0, col_f), (1, col_b)):
                        bw_parts[d].append((b_t * w_t[d][r:r + 1, :]).astype(BF16))
                        col_parts[d].append(col)
                y[c, q] = _dot(jnp.concatenate(m_parts, axis=1), x2) + xpb.astype(F32) * dskip_ref[:, sl]
                for d in range(2):
                    st[d, c, q] = _dot(jnp.concatenate(bw_parts[d], axis=1), x2)
                    cols[d, c, q] = col_parts[d]
                    cdp[d, c, q] = jnp.where(low_half, cd[d][2 * q:2 * q + 1, :], cd[d][2 * q + 1:2 * q + 2, :])

        for d in range(2):
            order = range(nc) if d == 0 else range(nc - 1, -1, -1)
            for q in range(hg // 2):
                sl = slice(q * LANES, (q + 1) * LANES)
                h = h0_ref[s, d, sl, :].T if has_h0 else None
                for c in order:
                    if h is None:
                        h = st[d, c, q]
                    else:
                        ecs = jnp.exp2(jnp.where(low_half, cols[d, c, q][0], cols[d, c, q][1]))
                        y[c, q] = y[c, q] + _dot(c16[c], h.astype(BF16)) * ecs
                        h = h * cdp[d, c, q] + st[d, c, q]
                if emit_state:
                    hf_ref[s, d, sl, :] = h.T
        for c in range(nc):
            for q in range(hg // 2):
                t0 = s * seq_len + c * CHUNK
                y_ref[t0:t0 + CHUNK, q * LANES:(q + 1) * LANES] = y[c, q].astype(BF16)

    if nc <= SHORT_SCAN_CHUNKS:
        for s in range(seqs):
            scan_short(s)
        return

    y_ref[...] = jnp.zeros(y_ref.shape, y_ref.dtype)
    for s in range(seqs):
        for d in range(2):
            for q in range(hg // 2):
                sl = slice(q * LANES, (q + 1) * LANES)
                if has_h0:
                    ht_ref[s, d, :, sl] = h0_ref[s, d, sl, :].T
                else:
                    ht_ref[s, d, :, sl] = jnp.zeros((D_STATE, LANES), F32)

    def chunk_step(s, d, c):
        mask = tril if d == 0 else triu
        t0 = s * seq_len + pl.multiple_of(c * CHUNK, CHUNK)
        dt16 = dt_ref[s * nc + c]
        da16 = dt16 * a2
        res = _dot(jnp.concatenate(_split3(da16), axis=0), sum_tot_b[d])
        sums = res[0:2 * hg] + res[2 * hg:4 * hg] + res[4 * hg:6 * hg]
        cs_t = sums[d * hg:(d + 1) * hg, 0:CHUNK]
        tot = sums[d * hg:(d + 1) * hg, CHUNK:2 * CHUNK]
        dt8 = dt16[d * hg:(d + 1) * hg]
        da8 = da16[d * hg:(d + 1) * hg]
        w_t = dt8 * jnp.exp2(tot - cs_t)
        cd = jnp.exp2(tot)
        src_t = cs_t - jnp.log(dt8) * LOG2E
        bc = b_ref[pl.ds(t0, CHUNK), :]
        cb16 = c_ref[pl.ds(t0, CHUNK), :]
        cb = _dot_nt(cb16, bc)
        b_t = bc.astype(F32).T

        for q in range(hg // 2):
            sl = slice(q * LANES, (q + 1) * LANES)
            xpb = xs_ref[pl.ds(t0, CHUNK), sl]
            x2 = jnp.concatenate([xpb * low_b, xpb * high_b], axis=0)
            m_parts, bw_parts, c_parts = [], [], []
            for r in (2 * q, 2 * q + 1):
                cs_col = jnp.sum(jnp.where(mask, da8[r:r + 1, :], 0.0), axis=1, keepdims=True)
                decay = jnp.where(mask, jnp.exp2(cs_col - src_t[r:r + 1, :]), 0.0)
                m_parts.append((cb * decay).astype(BF16))
                bw_parts.append((b_t * w_t[r:r + 1, :]).astype(BF16))
                c_parts.append(cs_col)
            y = _dot(jnp.concatenate(m_parts, axis=1), x2) + y_ref[pl.ds(t0, CHUNK), sl].astype(F32)
            st = _dot(jnp.concatenate(bw_parts, axis=1), x2)
            htp = ht_ref[s, d, :, sl]
            ecs = jnp.exp2(jnp.where(low_half, c_parts[0], c_parts[1]))
            cdp = jnp.where(low_half, cd[2 * q:2 * q + 1, :], cd[2 * q + 1:2 * q + 2, :])
            y = y + _dot(cb16, htp.astype(BF16)) * ecs
            ht_ref[s, d, :, sl] = htp * cdp + st
            if d == 0:
                y = y + xpb.astype(F32) * dskip_ref[:, sl]
            y_ref[pl.ds(t0, CHUNK), sl] = y.astype(BF16)

    def body(i, carry):
        for s in range(seqs):
            chunk_step(s, 0, i)
            chunk_step(s, 1, nc - 1 - i)
        return carry

    lax.fori_loop(0, nc, body, 0, unroll=2)

    if emit_state:
        for s in range(seqs):
            for d in range(2):
                for q in range(hg // 2):
                    sl = slice(q * LANES, (q + 1) * LANES)
                    hf_ref[s, d, sl, :] = ht_ref[s, d, :, sl].T


def _ssd_scan(xs, bm, cm, dt, alog, dskip, seq_len, seqs, ssd_index, h0=None, state_out=None):
    t_tokens = xs.shape[0]
    rows = seq_len * seqs
    nb = t_tokens // rows
    gw = HEADS_PER_GROUP * HEAD_DIM
    nc = seq_len // CHUNK
    in_specs = [
        pl.BlockSpec((rows, gw), lambda b, g: (b, g)),
        pl.BlockSpec((rows, D_STATE), lambda b, g: (b, g)),
        pl.BlockSpec((rows, D_STATE), lambda b, g: (b, g)),
        pl.BlockSpec((seqs * nc, 2 * HEADS_PER_GROUP, LANES), lambda b, g: (b, g, 0)),
        pl.BlockSpec((None, 2 * HEADS_PER_GROUP, LANES), lambda b, g: (ssd_index, g, 0)),
        pl.BlockSpec((None, 1, gw), lambda b, g: (ssd_index, 0, g)),
    ]
    args = [xs, bm, cm, dt, alog, dskip]
    has_h0 = h0 is not None
    if has_h0:
        in_specs.append(pl.BlockSpec((seqs, None, 2, None, gw, D_STATE),
                                     lambda b, g: (b, ssd_index, 0, g, 0, 0)))
        args.append(h0)
    emit_state = state_out is not None
    has_alias = emit_state and not isinstance(state_out, jax.ShapeDtypeStruct)
    out_specs = [pl.BlockSpec((rows, gw), lambda b, g: (b, g))]
    out_shape = [jax.ShapeDtypeStruct((t_tokens, D_INNER), BF16)]
    aliases = {}
    if emit_state:
        if has_alias:
            in_specs.append(pl.BlockSpec(memory_space=pl.ANY))
            args.append(state_out)
            aliases = {len(args) - 1: 1}
        state_sds = jax.ShapeDtypeStruct(state_out.shape, state_out.dtype)
        out_specs.append(pl.BlockSpec((seqs, None, 2, None, gw, D_STATE),
                                      lambda b, g: (b, ssd_index, 0, g, 0, 0)))
        out_shape.append(state_sds)
    res = pl.pallas_call(
        functools.partial(_ssd_scan_kernel, seq_len=seq_len, seqs=seqs, has_h0=has_h0,
                          emit_state=emit_state, has_alias=has_alias),
        grid=(nb, N_GROUPS),
        in_specs=in_specs,
        out_specs=out_specs,
        out_shape=out_shape,
        scratch_shapes=[pltpu.VMEM((seqs, 2, D_STATE, gw), F32)],
        input_output_aliases=aliases,
        compiler_params=_params("parallel", "parallel"),
        name="ssd_scan",
    )(*args)
    return res if emit_state else (res[0], None)


def _mlp_input(x, mod_ref, nw_ref):
    h = _rms(x) * nw_ref[...]
    return (h * (1.0 + mod_ref[4:5, :]) + mod_ref[3:4, :]).astype(BF16)


def _ssd_out_kernel(y_ref, z_ref, x_ref, mod_ref, nw_ref, wo_ref, nwm_ref, o_ref, hb_ref, wbf_ref):
    @pl.when(pl.program_id(0) == 0)
    def _():
        nw_cols = jnp.concatenate([nw_ref[...]] * (XBC_TILE // LANES), axis=1)
        for j in range(0, D_MODEL, XBC_TILE):
            wbf_ref[:, j:j + XBC_TILE] = (wo_ref[:, j:j + XBC_TILE] * nw_cols).astype(BF16)

    for r0 in range(0, x_ref.shape[0], SSD_OUT_ROWS):
        rs = slice(r0, r0 + SSD_OUT_ROWS)
        g = y_ref[rs, :] * _silu(z_ref[rs, :])
        gf = g.astype(F32)
        scale = lax.rsqrt(jnp.mean(gf * gf, axis=-1, keepdims=True) + EPS)
        x1 = x_ref[rs, :] + mod_ref[2:3, :] * (_dot(g, wbf_ref[...]) * scale)
        o_ref[rs, :] = x1
        hb_ref[rs, :] = _mlp_input(x1, mod_ref, nwm_ref)


def _ssd_out(y, z, x, mods, layer, j, first_row, tokens_per_cond, norm_w, w_out, norm_mlp):
    t_tokens = x.shape[0]
    tm = SSD_OUT_TILE
    tok = lambda n: pl.BlockSpec((tm, n), lambda t: (t, 0))
    return pl.pallas_call(
        _ssd_out_kernel,
        grid=(t_tokens // tm,),
        in_specs=[tok(D_INNER), tok(D_INNER), tok(D_MODEL),
                  _mod_spec(layer, tokens_per_cond // tm, first_row),
                  _layer_spec(j, (D_INNER, LANES)), _layer_spec(j, (D_INNER, D_MODEL)),
                  _layer_spec(layer, (1, D_MODEL))],
        out_specs=[tok(D_MODEL), tok(D_MODEL)],
        out_shape=[jax.ShapeDtypeStruct((t_tokens, D_MODEL), F32),
                   jax.ShapeDtypeStruct((t_tokens, D_MODEL), BF16)],
        scratch_shapes=[pltpu.VMEM((D_INNER, D_MODEL), BF16)],
        compiler_params=_params("arbitrary"),
        name="ssd_out",
    )(y, z, x, mods, norm_w, w_out, norm_mlp)


def _mlp_kernel(x_ref, hb_ref, mod_ref, w1_ref, w2_ref, fw_ref, o_ref, acc_ref, *, final):
    k = pl.program_id(1)
    last = pl.num_programs(1) - 1

    def block():
        a = jnp.maximum(_dot(hb_ref[...], w1_ref[...].astype(BF16)), 0.0)
        return _dot((a * a).astype(BF16), w2_ref[...].astype(BF16))

    @pl.when(k == 0)
    def _():
        acc_ref[...] = block()

    @pl.when(jnp.logical_and(k > 0, k < last))
    def _():
        acc_ref[...] += block()

    @pl.when(k == last)
    def _():
        x2 = x_ref[...] + mod_ref[5:6, :] * (acc_ref[...] + block())
        if final:
            x2 = _rms(x2) * fw_ref[...]
        o_ref[...] = x2


def _mlp(x, hb, mods, layer, first_row, tokens_per_cond, w1, w2, fw, final):
    t_tokens = x.shape[0]
    tm = MLP_TILE
    assert D_FF // FF_TILE >= 2
    tok = pl.BlockSpec((tm, D_MODEL), lambda t, k: (t, 0))
    return pl.pallas_call(
        functools.partial(_mlp_kernel, final=final),
        grid=(t_tokens // tm, D_FF // FF_TILE),
        in_specs=[tok, tok, _mod_spec(layer, tokens_per_cond // tm, first_row),
                  pl.BlockSpec((None, D_MODEL, FF_TILE), lambda t, k: (layer, 0, k)),
                  pl.BlockSpec((None, FF_TILE, D_MODEL), lambda t, k: (layer, k, 0)),
                  _const_spec((1, D_MODEL))],
        out_specs=tok,
        out_shape=jax.ShapeDtypeStruct((t_tokens, D_MODEL), F32),
        scratch_shapes=[pltpu.VMEM((tm, D_MODEL), F32)],
        compiler_params=_params("parallel", "arbitrary"),
        name="mlp",
    )(x, hb, mods, w1, w2, fw)


@functools.lru_cache(maxsize=None)
def _dft_matrices(n):
    k = np.arange(n)
    ang = 2.0 * np.pi * ((k[:, None] * k[None, :]) % n) / n
    return np.cos(ang) / np.sqrt(n), np.sin(ang) / np.sqrt(n)


def _fnet_kernel(x_ref, mod_ref, nw_ref, chan_ref, seqm_ref, wf_ref, bf_ref, nwm_ref, o_ref, hb_ref,
                 f_ref, wbf_ref, *, seq_len):
    @pl.when(pl.program_id(0) == 0)
    def _():
        _cast_weight(wbf_ref, wf_ref, D_MODEL)

    x = x_ref[...]
    h = _rms(x) * nw_ref[...]
    hb = (h * (1.0 + mod_ref[1:2, :]) + mod_ref[0:1, :]).astype(BF16)
    for g in range(N_FGROUPS):
        cols = slice(g * D_FG, (g + 1) * D_FG)
        p = _dot(hb[:, cols], chan_ref[...]).astype(BF16)
        for r0 in range(0, x_ref.shape[0], seq_len):
            rs = slice(r0, r0 + seq_len)
            stacked = jnp.concatenate([p[rs, :D_FG], p[rs, D_FG:]], axis=0)
            f_ref[rs, cols] = _dot(seqm_ref[...], stacked).astype(BF16)
    x1 = x + mod_ref[2:3, :] * (_dot(f_ref[...], wbf_ref[...]) + bf_ref[...])
    o_ref[...] = x1
    hb_ref[...] = _mlp_input(x1, mod_ref, nwm_ref)


def _fnet(x, mods, layer, j, first_row, seqs_per_cond, seq_len, seqs, norm_w, w_f, b_f, norm_mlp):
    t_tokens = x.shape[0]
    rows = seq_len * seqs
    cc, sc = _dft_matrices(D_FG)
    cl, sl = _dft_matrices(seq_len)
    chan = jnp.asarray(np.concatenate([cc, sc], axis=1), F32).astype(BF16)
    seqm = jnp.asarray(np.concatenate([cl, -sl], axis=1), F32).astype(BF16)
    tok = pl.BlockSpec((rows, D_MODEL), lambda t: (t, 0))
    return pl.pallas_call(
        functools.partial(_fnet_kernel, seq_len=seq_len),
        grid=(t_tokens // rows,),
        in_specs=[tok, _mod_spec(layer, seqs_per_cond // seqs, first_row), _layer_spec(layer, (1, D_MODEL)),
                  _const_spec((D_FG, 2 * D_FG)), _const_spec((seq_len, 2 * seq_len)),
                  _layer_spec(j, (D_MODEL, D_MODEL)), _layer_spec(j, (1, D_MODEL)),
                  _layer_spec(layer, (1, D_MODEL))],
        out_specs=[tok, tok],
        out_shape=[jax.ShapeDtypeStruct((t_tokens, D_MODEL), F32),
                   jax.ShapeDtypeStruct((t_tokens, D_MODEL), BF16)],
        scratch_shapes=[pltpu.VMEM((rows, D_MODEL), BF16), pltpu.VMEM((D_MODEL, D_MODEL), BF16)],
        compiler_params=_params("arbitrary"),
        name="fnet",
    )(x, mods, norm_w, chan, seqm, w_f, b_f, norm_mlp)


def _head_perm():
    return np.array([d * N_HEADS + g * HEADS_PER_GROUP + r
                     for g in range(N_GROUPS) for d in range(2) for r in range(HEADS_PER_GROUP)])


def kernel(x_prompt, x_sample, state_ssd, c, c_ctx, ada_w, ada_b, norm_mix_w, norm_mlp_w, ssd_w_in, ssd_conv_w, ssd_conv_b, ssd_dt_bias, ssd_a_log, ssd_d, ssd_norm_w, ssd_w_out, fno_w_out, fno_b_out, mlp_w1, mlp_w2, final_norm_w):
    batch, seq, _ = x_prompt.shape
    dec_batch, dec_seq, _ = x_sample.shape
    n_ssd = ssd_w_in.shape[0]

    cond = jnp.concatenate([c_ctx[None], c, jnp.zeros((N_COND - 1 - dec_batch, D_MODEL), F32)], axis=0)
    mods = _ada_modulation(cond, ada_w, ada_b)

    perm = _head_perm()
    lane_rep = lambda v: jnp.broadcast_to(v.reshape(n_ssd, 2 * N_HEADS)[:, perm][:, :, None],
                                          (n_ssd, 2 * N_HEADS, LANES))
    dtb = lane_rep(ssd_dt_bias)
    alog = lane_rep(ssd_a_log)
    dskip = jnp.repeat(ssd_d, HEAD_DIM, axis=1)[:, None, :]
    rows3 = lambda v: v[:, None, :]
    w_in_t = jnp.swapaxes(ssd_w_in, 1, 2)
    norm_mix, norm_mlp = rows3(norm_mix_w), rows3(norm_mlp_w)
    conv_b, fno_b = rows3(ssd_conv_b), rows3(fno_b_out)
    ssd_nw = jnp.broadcast_to(ssd_norm_w[:, :, None], (n_ssd, D_INNER, LANES))
    final_w = final_norm_w[None, :]

    gw = HEADS_PER_GROUP * HEAD_DIM
    new_state = jax.ShapeDtypeStruct((batch, n_ssd, 2, N_GROUPS, gw, D_STATE), F32)
    h0_all = state_ssd.reshape(dec_batch, n_ssd, 2, N_GROUPS, gw, D_STATE)

    streams = [
        dict(first_row=0, tokens_per_cond=batch * seq, seqs_per_cond=batch, seg=seq, seq_len=seq,
             scan_seqs=SCAN_SEQS if batch % SCAN_SEQS == 0 else 1,
             fnet_seqs=FNET_SEQS if batch % FNET_SEQS == 0 else 1),
        dict(first_row=1, tokens_per_cond=dec_seq, seqs_per_cond=1, seg=GRID_W, seq_len=dec_seq, scan_seqs=1,
             fnet_seqs=1),
    ]
    acts = [x_prompt.reshape(batch * seq, D_MODEL), x_sample.reshape(dec_batch * dec_seq, D_MODEL)]
    for i in range(DEPTH):
        j = i // 2
        for s, st in enumerate(streams):
            x = acts[s]
            fr, tpc = st["first_row"], st["tokens_per_cond"]
            if i % 2 == 0:
                z, xc, bm, cm, dt = _ssd_in(x, mods, i, j, fr, tpc, st["seg"], norm_mix, w_in_t,
                                            ssd_conv_w, conv_b, dtb)
                if s == 0:
                    y, new_state = _ssd_scan(xc, bm, cm, dt, alog, dskip, st["seq_len"], st["scan_seqs"], j,
                                             state_out=new_state)
                else:
                    y, _ = _ssd_scan(xc, bm, cm, dt, alog, dskip, st["seq_len"], st["scan_seqs"], j, h0=h0_all)
                x, hb = _ssd_out(y, z, x, mods, i, j, fr, tpc, ssd_nw, ssd_w_out, norm_mlp)
            else:
                x, hb = _fnet(x, mods, i, j, fr, st["seqs_per_cond"], st["seq_len"], st["fnet_seqs"], norm_mix,
                              fno_w_out, fno_b, norm_mlp)
            acts[s] = _mlp(x, hb, mods, i, fr, tpc, mlp_w1, mlp_w2, final_w, i == DEPTH - 1)

    y_prompt = acts[0].reshape(batch, seq, D_MODEL)
    y_sample = acts[1].reshape(dec_batch, dec_seq, D_MODEL)
    new_state_ssd = new_state.reshape(batch, n_ssd, 2, N_HEADS, HEAD_DIM, D_STATE)
    return (y_prompt, y_sample, new_state_ssd)
```

```python
import functools

import numpy as np
import jax
import jax.numpy as jnp
from jax import lax
from jax.experimental import pallas as pl
from jax.experimental.pallas import tpu as pltpu

D_MODEL = 1024
DEPTH = 4
GRID_W = 64
D_INNER = 2048
HEAD_DIM = 64
N_HEADS = 32
N_GROUPS = 4
HEADS_PER_GROUP = N_HEADS // N_GROUPS
HG_SHIFT = HEADS_PER_GROUP.bit_length() - 1
D_STATE = 128
CHUNK = 128
D_BC = N_GROUPS * D_STATE
D_XBC = D_INNER + 2 * D_BC
D_ZX = D_INNER + D_XBC
D_IN_PROJ = D_ZX + 2 * N_HEADS
N_FGROUPS = 4
D_FG = D_MODEL // N_FGROUPS
D_FF = 4 * D_MODEL
EPS = 1e-6
N_COND = 8
LOG2E = 1.4426950408889634

LANES = 128
SSD_IN_TILE = 256
SSD_OUT_TILE = 512
SSD_OUT_ROWS = 256
MLP_TILE = 1024
SCAN_SEQS = 4
FNET_SEQS = 4
SHORT_SCAN_CHUNKS = 8
XBC_TILE = 512
FF_TILE = 1024
ADA_TILE = 1536
VMEM_LIMIT = 56 * 1024 * 1024

F32 = jnp.float32
BF16 = jnp.bfloat16


def _dot(a, b):
    return jnp.dot(a, b, preferred_element_type=F32)


def _dot_nt(a, b):
    return lax.dot_general(a, b, (((1,), (1,)), ((), ())), preferred_element_type=F32)


def _silu(x):
    h = 0.5 * x
    return h + h * jnp.tanh(h)


def _rms(x):
    return x * lax.rsqrt(jnp.mean(x * x, axis=-1, keepdims=True) + EPS)


def _const_spec(shape):
    nd = len(shape)
    return pl.BlockSpec(shape, lambda *_: (0,) * nd, pipeline_mode=pl.Buffered(1))


def _layer_spec(layer, tail):
    nd = len(tail)
    return pl.BlockSpec((None,) + tuple(tail), lambda *_: (layer,) + (0,) * nd,
                        pipeline_mode=pl.Buffered(1))


def _mod_spec(layer, tiles_per_cond, first_row):
    return pl.BlockSpec((None, None, 6, D_MODEL),
                        lambda t, *_: (layer, first_row + t // tiles_per_cond, 0, 0))


def _params(*semantics):
    return pltpu.CompilerParams(dimension_semantics=semantics, vmem_limit_bytes=VMEM_LIMIT)


def _cast_weight(dst_ref, src_ref, n_cols, tile=512):
    for j in range(0, n_cols, tile):
        dst_ref[:, j:j + tile] = src_ref[:, j:j + tile].astype(BF16)


def _ada_kernel(c_ref, w_ref, b_ref, o_ref):
    c = c_ref[...]
    s = _silu(c).astype(BF16)
    o_ref[...] = _dot(s, w_ref[...].astype(BF16)) + b_ref[...]


def _ada_modulation(cond, ada_w, ada_b):
    n_out = 6 * D_MODEL
    out = pl.pallas_call(
        _ada_kernel,
        grid=(DEPTH, n_out // ADA_TILE),
        in_specs=[
            pl.BlockSpec((N_COND, D_MODEL), lambda i, n: (0, 0)),
            pl.BlockSpec((None, D_MODEL, ADA_TILE), lambda i, n: (i, 0, n)),
            pl.BlockSpec((None, 1, ADA_TILE), lambda i, n: (i, 0, n)),
        ],
        out_specs=pl.BlockSpec((None, N_COND, ADA_TILE), lambda i, n: (i, 0, n)),
        out_shape=jax.ShapeDtypeStruct((DEPTH, N_COND, n_out), F32),
        compiler_params=_params("parallel", "parallel"),
        name="ada_mod",
    )(cond, ada_w, ada_b.reshape(DEPTH, 1, n_out))
    return out.reshape(DEPTH, N_COND, 6, D_MODEL)


def _ssd_in_kernel(x_ref, mod_ref, nw_ref, win_ref, cw_ref, cb_ref, dtb_ref,
                   z_ref, xs_ref, b_ref, c_ref, dt_ref, wbf_ref, wdt_ref, *, seg):
    @pl.when(pl.program_id(0) == 0)
    def _():
        for r0 in range(0, D_ZX, XBC_TILE):
            wbf_ref[r0:r0 + XBC_TILE, :] = win_ref[r0:r0 + XBC_TILE, :].astype(BF16)
        n = lax.broadcasted_iota(jnp.int32, (2 * N_HEADS, 2 * N_HEADS), 0)
        k = lax.broadcasted_iota(jnp.int32, (2 * N_HEADS, 2 * N_HEADS), 1)
        src = (((n >> HG_SHIFT) & 1) * N_HEADS + (n >> (HG_SHIFT + 1)) * HEADS_PER_GROUP
               + (n & (HEADS_PER_GROUP - 1)))
        pick = jnp.where(k == src, 1.0, 0.0).astype(BF16)
        wdt_ref[...] = _dot(pick, win_ref[D_ZX:D_IN_PROJ, :].astype(BF16)).astype(BF16)

    tm = x_ref.shape[0]
    h = _rms(x_ref[...]) * nw_ref[...]
    h = h * (1.0 + mod_ref[1:2, :]) + mod_ref[0:1, :]
    hb = h.astype(BF16)
    for j in range(D_INNER // XBC_TILE):
        cols = slice(j * XBC_TILE, (j + 1) * XBC_TILE)
        z_ref[:, cols] = _dot_nt(hb, wbf_ref[cols, :]).astype(BF16)

    row = lax.broadcasted_iota(jnp.int32, (tm, XBC_TILE), 0) & (seg - 1)
    first = row == 0
    last = row == seg - 1
    for j in range(D_XBC // XBC_TILE):
        cols = slice(j * XBC_TILE, (j + 1) * XBC_TILE)
        acc = _dot_nt(hb, wbf_ref[D_INNER + j * XBC_TILE:D_INNER + (j + 1) * XBC_TILE, :])
        up = jnp.where(first, 0.0, pltpu.roll(acc, 1, axis=0))
        dn = jnp.where(last, 0.0, pltpu.roll(acc, tm - 1, axis=0))
        o = up * cw_ref[0:1, cols] + acc * cw_ref[1:2, cols] + dn * cw_ref[2:3, cols] + cb_ref[:, cols]
        o = _silu(o).astype(BF16)
        if j < D_INNER // XBC_TILE:
            xs_ref[:, cols] = o
        elif j == D_INNER // XBC_TILE:
            b_ref[...] = o
        else:
            c_ref[...] = o

    dtT = _dot_nt(wdt_ref[...], hb)
    for k in range(tm // CHUNK):
        v = dtT[:, k * CHUNK:(k + 1) * CHUNK] + dtb_ref[...]
        dt_ref[k] = jnp.maximum(v, 0.0) + jnp.log1p(jnp.exp(-jnp.abs(v)))


def _ssd_in(x, mods, layer, j, first_row, tokens_per_cond, seg, norm_w, w_in, conv_w, conv_b, dtb):
    t_tokens = x.shape[0]
    tm = SSD_IN_TILE
    tok = lambda n: pl.BlockSpec((tm, n), lambda t: (t, 0))
    return pl.pallas_call(
        functools.partial(_ssd_in_kernel, seg=seg),
        grid=(t_tokens // tm,),
        in_specs=[
            tok(D_MODEL),
            _mod_spec(layer, tokens_per_cond // tm, first_row),
            _layer_spec(layer, (1, D_MODEL)),
            _layer_spec(j, (D_IN_PROJ, D_MODEL)),
            _layer_spec(j, (3, D_XBC)),
            _layer_spec(j, (1, D_XBC)),
            _layer_spec(j, (2 * N_HEADS, LANES)),
        ],
        out_specs=[
            tok(D_INNER), tok(D_INNER), tok(D_BC), tok(D_BC),
            pl.BlockSpec((tm // CHUNK, 2 * N_HEADS, LANES), lambda t: (t, 0, 0)),
        ],
        out_shape=[
            jax.ShapeDtypeStruct((t_tokens, D_INNER), BF16),
            jax.ShapeDtypeStruct((t_tokens, D_INNER), BF16),
            jax.ShapeDtypeStruct((t_tokens, D_BC), BF16),
            jax.ShapeDtypeStruct((t_tokens, D_BC), BF16),
            jax.ShapeDtypeStruct((t_tokens // CHUNK, 2 * N_HEADS, LANES), F32),
        ],
        scratch_shapes=[pltpu.VMEM((D_ZX, D_MODEL), BF16), pltpu.VMEM((2 * N_HEADS, D_MODEL), BF16)],
        compiler_params=_params("arbitrary"),
        name="ssd_in",
    )(x, mods, norm_w, w_in, conv_w, conv_b, dtb)


def _split3(a):
    hi = a.astype(BF16)
    r = a - hi.astype(F32)
    mid = r.astype(BF16)
    lo = (r - mid.astype(F32)).astype(BF16)
    return hi, mid, lo


def _dot3(a, m):
    hi, mid, lo = _split3(a)
    return _dot(hi, m) + _dot(mid, m) + _dot(lo, m)


def _ssd_scan_kernel(*refs, seq_len, seqs, has_h0, emit_state, has_alias):
    xs_ref, b_ref, c_ref, dt_ref, alog_ref, dskip_ref = refs[:6]
    pos = 6
    h0_ref = refs[pos] if has_h0 else None
    pos += int(has_h0) + int(has_alias)
    y_ref = refs[pos]
    hf_ref = refs[pos + 1] if emit_state else None
    ht_ref = refs[-1]

    nc = seq_len // CHUNK
    hg = HEADS_PER_GROUP
    rows = lax.broadcasted_iota(jnp.int32, (CHUNK, CHUNK), 0)
    lanes = lax.broadcasted_iota(jnp.int32, (CHUNK, CHUNK), 1)
    tril = rows >= lanes
    triu = rows <= lanes
    tril_b = jnp.where(tril, 1.0, 0.0).astype(BF16)
    triu_b = jnp.where(triu, 1.0, 0.0).astype(BF16)
    ones_b = jnp.ones((CHUNK, CHUNK), BF16)
    low_half = lanes < HEAD_DIM
    low_b = jnp.where(low_half, 1.0, 0.0).astype(BF16)
    high_b = jnp.where(low_half, 0.0, 1.0).astype(BF16)
    a2 = -jnp.exp(alog_ref[...]) * LOG2E

    sum_tot_b = [jnp.concatenate([triu_b, ones_b], axis=1), jnp.concatenate([tril_b, ones_b], axis=1)]

    def scan_short(s):
        below = rows > lanes
        above = rows < lanes
        y, st, cols, cdp, c16 = {}, {}, {}, {}, {}
        for c in range(nc):
            t0 = s * seq_len + c * CHUNK
            dt16 = dt_ref[s * nc + c]
            da16 = dt16 * a2
            parts = jnp.concatenate(_split3(da16), axis=0)
            dts, das, cs_t, w_t, cd, src_t = [], [], [], [], [], []
            for d in range(2):
                res = _dot(parts, sum_tot_b[d])
                sums = (res[0:2 * hg] + res[2 * hg:4 * hg] + res[4 * hg:6 * hg])[d * hg:(d + 1) * hg]
                dts.append(dt16[d * hg:(d + 1) * hg])
                das.append(da16[d * hg:(d + 1) * hg])
                cs_t.append(sums[:, 0:CHUNK])
                tot = sums[:, CHUNK:2 * CHUNK]
                w_t.append(dts[d] * jnp.exp2(tot - cs_t[d]))
                cd.append(jnp.exp2(tot))
                src_t.append(cs_t[d] - jnp.log(dts[d]) * LOG2E)
            diag_t = jnp.log(dts[0] + dts[1]) * LOG2E
            bc = b_ref[t0:t0 + CHUNK, :]
            c16[c] = c_ref[t0:t0 + CHUNK, :]
            cb = _dot_nt(c16[c], bc)
            b_t = bc.astype(F32).T.astype(BF16)
            w16 = [w.astype(BF16) for w in w_t]
            for q in range(hg // 2):
                sl = slice(q * LANES, (q + 1) * LANES)
                xpb = xs_ref[t0:t0 + CHUNK, sl]
                x2 = jnp.concatenate([xpb * low_b, xpb * high_b], axis=0)
                m_parts, bw_parts, col_parts = [], ([], []), ([], [])
                for r in (2 * q, 2 * q + 1):
                    col_f = jnp.sum(jnp.where(tril, das[0][r:r + 1, :], 0.0), axis=1, keepdims=True)
                    col_b = jnp.sum(jnp.where(triu, das[1][r:r + 1, :], 0.0), axis=1, keepdims=True)
                    arg = jnp.where(below, col_f - src_t[0][r:r + 1, :],
                                    jnp.where(above, col_b - src_t[1][r:r + 1, :], diag_t[r:r + 1, :]))
                    m_parts.append((cb * jnp.exp2(arg)).astype(BF16))
                    for d, col in ((0, col_f), (1, col_b)):
                        bw_parts[d].append(b_t * w16[d][r:r + 1, :])
                        col_parts[d].append(col)
                y[c, q] = _dot(jnp.concatenate(m_parts, axis=1), x2) + xpb.astype(F32) * dskip_ref[:, sl]
                for d in range(2):
                    st[d, c, q] = _dot(jnp.concatenate(bw_parts[d], axis=1), x2)
                    cols[d, c, q] = col_parts[d]
                    cdp[d, c, q] = jnp.where(low_half, cd[d][2 * q:2 * q + 1, :], cd[d][2 * q + 1:2 * q + 2, :])

        for d in range(2):
            order = range(nc) if d == 0 else range(nc - 1, -1, -1)
            for q in range(hg // 2):
                sl = slice(q * LANES, (q + 1) * LANES)
                h = h0_ref[s, d, sl, :].T if has_h0 else None
                for c in order:
                    if h is None:
                        h = st[d, c, q]
                    else:
                        ecs = jnp.exp2(jnp.where(low_half, cols[d, c, q][0], cols[d, c, q][1]))
                        y[c, q] = y[c, q] + _dot(c16[c], h.astype(BF16)) * ecs
                        h = h * cdp[d, c, q] + st[d, c, q]
                if emit_state:
                    hf_ref[s, d, sl, :] = h.T
        for c in range(nc):
            for q in range(hg // 2):
                t0 = s * seq_len + c * CHUNK
                y_ref[t0:t0 + CHUNK, q * LANES:(q + 1) * LANES] = y[c, q].astype(BF16)

    if nc <= SHORT_SCAN_CHUNKS:
        for s in range(seqs):
            scan_short(s)
        return

    y_ref[...] = jnp.zeros(y_ref.shape, y_ref.dtype)
    for s in range(seqs):
        for d in range(2):
            for q in range(hg // 2):
                sl = slice(q * LANES, (q + 1) * LANES)
                if has_h0:
                    ht_ref[s, d, :, sl] = h0_ref[s, d, sl, :].T
                else:
                    ht_ref[s, d, :, sl] = jnp.zeros((D_STATE, LANES), F32)

    def chunk_step(s, d, c):
        mask = tril if d == 0 else triu
        t0 = s * seq_len + pl.multiple_of(c * CHUNK, CHUNK)
        dt16 = dt_ref[s * nc + c]
        da16 = dt16 * a2
        res = _dot(jnp.concatenate(_split3(da16), axis=0), sum_tot_b[d])
        sums = res[0:2 * hg] + res[2 * hg:4 * hg] + res[4 * hg:6 * hg]
        cs_t = sums[d * hg:(d + 1) * hg, 0:CHUNK]
        tot = sums[d * hg:(d + 1) * hg, CHUNK:2 * CHUNK]
        dt8 = dt16[d * hg:(d + 1) * hg]
        da8 = da16[d * hg:(d + 1) * hg]
        w_t = dt8 * jnp.exp2(tot - cs_t)
        cd = jnp.exp2(tot)
        src_t = cs_t - jnp.log(dt8) * LOG2E
        bc = b_ref[pl.ds(t0, CHUNK), :]
        cb16 = c_ref[pl.ds(t0, CHUNK), :]
        cb = _dot_nt(cb16, bc)
        b_t = bc.astype(F32).T

        for q in range(hg // 2):
            sl = slice(q * LANES, (q + 1) * LANES)
            xpb = xs_ref[pl.ds(t0, CHUNK), sl]
            x2 = jnp.concatenate([xpb * low_b, xpb * high_b], axis=0)
            m_parts, bw_parts, c_parts = [], [], []
            for r in (2 * q, 2 * q + 1):
                cs_col = jnp.sum(jnp.where(mask, da8[r:r + 1, :], 0.0), axis=1, keepdims=True)
                decay = jnp.where(mask, jnp.exp2(cs_col - src_t[r:r + 1, :]), 0.0)
                m_parts.append((cb * decay).astype(BF16))
                bw_parts.append((b_t * w_t[r:r + 1, :]).astype(BF16))
                c_parts.append(cs_col)
            y = _dot(jnp.concatenate(m_parts, axis=1), x2) + y_ref[pl.ds(t0, CHUNK), sl].astype(F32)
            st = _dot(jnp.concatenate(bw_parts, axis=1), x2)
            htp = ht_ref[s, d, :, sl]
            ecs = jnp.exp2(jnp.where(low_half, c_parts[0], c_parts[1]))
            cdp = jnp.where(low_half, cd[2 * q:2 * q + 1, :], cd[2 * q + 1:2 * q + 2, :])
            y = y + _dot(cb16, htp.astype(BF16)) * ecs
            ht_ref[s, d, :, sl] = htp * cdp + st
            if d == 0:
                y = y + xpb.astype(F32) * dskip_ref[:, sl]
            y_ref[pl.ds(t0, CHUNK), sl] = y.astype(BF16)

    def body(i, carry):
        for s in range(seqs):
            chunk_step(s, 0, i)
            chunk_step(s, 1, nc - 1 - i)
        return carry

    lax.fori_loop(0, nc, body, 0, unroll=2)

    if emit_state:
        for s in range(seqs):
            for d in range(2):
                for q in range(hg // 2):
                    sl = slice(q * LANES, (q + 1) * LANES)
                    hf_ref[s, d, sl, :] = ht_ref[s, d, :, sl].T


def _ssd_scan(xs, bm, cm, dt, alog, dskip, seq_len, seqs, ssd_index, h0=None, state_out=None):
    t_tokens = xs.shape[0]
    rows = seq_len * seqs
    nb = t_tokens // rows
    gw = HEADS_PER_GROUP * HEAD_DIM
    nc = seq_len // CHUNK
    in_specs = [
        pl.BlockSpec((rows, gw), lambda b, g: (b, g)),
        pl.BlockSpec((rows, D_STATE), lambda b, g: (b, g)),
        pl.BlockSpec((rows, D_STATE), lambda b, g: (b, g)),
        pl.BlockSpec((seqs * nc, 2 * HEADS_PER_GROUP, LANES), lambda b, g: (b, g, 0)),
        pl.BlockSpec((None, 2 * HEADS_PER_GROUP, LANES), lambda b, g: (ssd_index, g, 0)),
        pl.BlockSpec((None, 1, gw), lambda b, g: (ssd_index, 0, g)),
    ]
    args = [xs, bm, cm, dt, alog, dskip]
    has_h0 = h0 is not None
    if has_h0:
        in_specs.append(pl.BlockSpec((seqs, None, 2, None, gw, D_STATE),
                                     lambda b, g: (b, ssd_index, 0, g, 0, 0)))
        args.append(h0)
    emit_state = state_out is not None
    has_alias = emit_state and not isinstance(state_out, jax.ShapeDtypeStruct)
    out_specs = [pl.BlockSpec((rows, gw), lambda b, g: (b, g))]
    out_shape = [jax.ShapeDtypeStruct((t_tokens, D_INNER), BF16)]
    aliases = {}
    if emit_state:
        if has_alias:
            in_specs.append(pl.BlockSpec(memory_space=pl.ANY))
            args.append(state_out)
            aliases = {len(args) - 1: 1}
        state_sds = jax.ShapeDtypeStruct(state_out.shape, state_out.dtype)
        out_specs.append(pl.BlockSpec((seqs, None, 2, None, gw, D_STATE),
                                      lambda b, g: (b, ssd_index, 0, g, 0, 0)))
        out_shape.append(state_sds)
    res = pl.pallas_call(
        functools.partial(_ssd_scan_kernel, seq_len=seq_len, seqs=seqs, has_h0=has_h0,
                          emit_state=emit_state, has_alias=has_alias),
        grid=(nb, N_GROUPS),
        in_specs=in_specs,
        out_specs=out_specs,
        out_shape=out_shape,
        scratch_shapes=[pltpu.VMEM((seqs, 2, D_STATE, gw), F32)],
        input_output_aliases=aliases,
        compiler_params=_params("parallel", "parallel"),
        name="ssd_scan",
    )(*args)
    return res if emit_state else (res[0], None)


def _mlp_input(x, mod_ref, nw_ref):
    h = _rms(x) * nw_ref[...]
    return (h * (1.0 + mod_ref[4:5, :]) + mod_ref[3:4, :]).astype(BF16)


def _ssd_out_kernel(y_ref, z_ref, x_ref, mod_ref, nw_ref, wo_ref, nwm_ref, o_ref, hb_ref, wbf_ref):
    @pl.when(pl.program_id(0) == 0)
    def _():
        nw_cols = jnp.concatenate([nw_ref[...]] * (XBC_TILE // LANES), axis=1)
        for j in range(0, D_MODEL, XBC_TILE):
            wbf_ref[:, j:j + XBC_TILE] = (wo_ref[:, j:j + XBC_TILE] * nw_cols).astype(BF16)

    for r0 in range(0, x_ref.shape[0], SSD_OUT_ROWS):
        rs = slice(r0, r0 + SSD_OUT_ROWS)
        g = y_ref[rs, :] * _silu(z_ref[rs, :])
        gf = g.astype(F32)
        scale = lax.rsqrt(jnp.mean(gf * gf, axis=-1, keepdims=True) + EPS)
        x1 = x_ref[rs, :] + mod_ref[2:3, :] * (_dot(g, wbf_ref[...]) * scale)
        o_ref[rs, :] = x1
        hb_ref[rs, :] = _mlp_input(x1, mod_ref, nwm_ref)


def _ssd_out(y, z, x, mods, layer, j, first_row, tokens_per_cond, norm_w, w_out, norm_mlp):
    t_tokens = x.shape[0]
    tm = SSD_OUT_TILE
    tok = lambda n: pl.BlockSpec((tm, n), lambda t: (t, 0))
    return pl.pallas_call(
        _ssd_out_kernel,
        grid=(t_tokens // tm,),
        in_specs=[tok(D_INNER), tok(D_INNER), tok(D_MODEL),
                  _mod_spec(layer, tokens_per_cond // tm, first_row),
                  _layer_spec(j, (D_INNER, LANES)), _layer_spec(j, (D_INNER, D_MODEL)),
                  _layer_spec(layer, (1, D_MODEL))],
        out_specs=[tok(D_MODEL), tok(D_MODEL)],
        out_shape=[jax.ShapeDtypeStruct((t_tokens, D_MODEL), F32),
                   jax.ShapeDtypeStruct((t_tokens, D_MODEL), BF16)],
        scratch_shapes=[pltpu.VMEM((D_INNER, D_MODEL), BF16)],
        compiler_params=_params("arbitrary"),
        name="ssd_out",
    )(y, z, x, mods, norm_w, w_out, norm_mlp)


def _mlp_kernel(x_ref, hb_ref, mod_ref, w1_ref, w2_ref, fw_ref, o_ref, acc_ref, *, final):
    k = pl.program_id(1)
    last = pl.num_programs(1) - 1

    def block():
        a = jnp.maximum(_dot(hb_ref[...], w1_ref[...].astype(BF16)), 0.0)
        return _dot((a * a).astype(BF16), w2_ref[...].astype(BF16))

    @pl.when(k == 0)
    def _():
        acc_ref[...] = block()

    @pl.when(jnp.logical_and(k > 0, k < last))
    def _():
        acc_ref[...] += block()

    @pl.when(k == last)
    def _():
        x2 = x_ref[...] + mod_ref[5:6, :] * (acc_ref[...] + block())
        if final:
            x2 = _rms(x2) * fw_ref[...]
        o_ref[...] = x2


def _mlp(x, hb, mods, layer, first_row, tokens_per_cond, w1, w2, fw, final):
    t_tokens = x.shape[0]
    tm = MLP_TILE
    assert D_FF // FF_TILE >= 2
    tok = pl.BlockSpec((tm, D_MODEL), lambda t, k: (t, 0))
    return pl.pallas_call(
        functools.partial(_mlp_kernel, final=final),
        grid=(t_tokens // tm, D_FF // FF_TILE),
        in_specs=[tok, tok, _mod_spec(layer, tokens_per_cond // tm, first_row),
                  pl.BlockSpec((None, D_MODEL, FF_TILE), lambda t, k: (layer, 0, k)),
                  pl.BlockSpec((None, FF_TILE, D_MODEL), lambda t, k: (layer, k, 0)),
                  _const_spec((1, D_MODEL))],
        out_specs=tok,
        out_shape=jax.ShapeDtypeStruct((t_tokens, D_MODEL), F32),
        scratch_shapes=[pltpu.VMEM((tm, D_MODEL), F32)],
        compiler_params=_params("parallel", "arbitrary"),
        name="mlp",
    )(x, hb, mods, w1, w2, fw)


@functools.lru_cache(maxsize=None)
def _dft_matrices(n):
    k = np.arange(n)
    ang = 2.0 * np.pi * ((k[:, None] * k[None, :]) % n) / n
    return np.cos(ang) / np.sqrt(n), np.sin(ang) / np.sqrt(n)


def _fnet_kernel(x_ref, mod_ref, nw_ref, chan_ref, seqm_ref, wf_ref, bf_ref, nwm_ref, o_ref, hb_ref,
                 f_ref, wbf_ref, *, seq_len):
    @pl.when(pl.program_id(0) == 0)
    def _():
        _cast_weight(wbf_ref, wf_ref, D_MODEL)

    x = x_ref[...]
    h = _rms(x) * nw_ref[...]
    hb = (h * (1.0 + mod_ref[1:2, :]) + mod_ref[0:1, :]).astype(BF16)
    for g in range(N_FGROUPS):
        cols = slice(g * D_FG, (g + 1) * D_FG)
        p = _dot(hb[:, cols], chan_ref[...]).astype(BF16)
        for r0 in range(0, x_ref.shape[0], seq_len):
            rs = slice(r0, r0 + seq_len)
            stacked = jnp.concatenate([p[rs, :D_FG], p[rs, D_FG:]], axis=0)
            f_ref[rs, cols] = _dot(seqm_ref[...], stacked).astype(BF16)
    x1 = x + mod_ref[2:3, :] * (_dot(f_ref[...], wbf_ref[...]) + bf_ref[...])
    o_ref[...] = x1
    hb_ref[...] = _mlp_input(x1, mod_ref, nwm_ref)


def _fnet(x, mods, layer, j, first_row, seqs_per_cond, seq_len, seqs, norm_w, w_f, b_f, norm_mlp):
    t_tokens = x.shape[0]
    rows = seq_len * seqs
    cc, sc = _dft_matrices(D_FG)
    cl, sl = _dft_matrices(seq_len)
    chan = jnp.asarray(np.concatenate([cc, sc], axis=1), F32).astype(BF16)
    seqm = jnp.asarray(np.concatenate([cl, -sl], axis=1), F32).astype(BF16)
    tok = pl.BlockSpec((rows, D_MODEL), lambda t: (t, 0))
    return pl.pallas_call(
        functools.partial(_fnet_kernel, seq_len=seq_len),
        grid=(t_tokens // rows,),
        in_specs=[tok, _mod_spec(layer, seqs_per_cond // seqs, first_row), _layer_spec(layer, (1, D_MODEL)),
                  _const_spec((D_FG, 2 * D_FG)), _const_spec((seq_len, 2 * seq_len)),
                  _layer_spec(j, (D_MODEL, D_MODEL)), _layer_spec(j, (1, D_MODEL)),
                  _layer_spec(layer, (1, D_MODEL))],
        out_specs=[tok, tok],
        out_shape=[jax.ShapeDtypeStruct((t_tokens, D_MODEL), F32),
                   jax.ShapeDtypeStruct((t_tokens, D_MODEL), BF16)],
        scratch_shapes=[pltpu.VMEM((rows, D_MODEL), BF16), pltpu.VMEM((D_MODEL, D_MODEL), BF16)],
        compiler_params=_params("arbitrary"),
        name="fnet",
    )(x, mods, norm_w, chan, seqm, w_f, b_f, norm_mlp)


def _head_perm():
    return np.array([d * N_HEADS + g * HEADS_PER_GROUP + r
                     for g in range(N_GROUPS) for d in range(2) for r in range(HEADS_PER_GROUP)])


def kernel(x_prompt, x_sample, state_ssd, c, c_ctx, ada_w, ada_b, norm_mix_w, norm_mlp_w, ssd_w_in, ssd_conv_w, ssd_conv_b, ssd_dt_bias, ssd_a_log, ssd_d, ssd_norm_w, ssd_w_out, fno_w_out, fno_b_out, mlp_w1, mlp_w2, final_norm_w):
    batch, seq, _ = x_prompt.shape
    dec_batch, dec_seq, _ = x_sample.shape
    n_ssd = ssd_w_in.shape[0]

    cond = jnp.concatenate([c_ctx[None], c, jnp.zeros((N_COND - 1 - dec_batch, D_MODEL), F32)], axis=0)
    mods = _ada_modulation(cond, ada_w, ada_b)

    perm = _head_perm()
    lane_rep = lambda v: jnp.broadcast_to(v.reshape(n_ssd, 2 * N_HEADS)[:, perm][:, :, None],
                                          (n_ssd, 2 * N_HEADS, LANES))
    dtb = lane_rep(ssd_dt_bias)
    alog = lane_rep(ssd_a_log)
    dskip = jnp.repeat(ssd_d, HEAD_DIM, axis=1)[:, None, :]
    rows3 = lambda v: v[:, None, :]
    w_in_t = jnp.swapaxes(ssd_w_in, 1, 2)
    norm_mix, norm_mlp = rows3(norm_mix_w), rows3(norm_mlp_w)
    conv_b, fno_b = rows3(ssd_conv_b), rows3(fno_b_out)
    ssd_nw = jnp.broadcast_to(ssd_norm_w[:, :, None], (n_ssd, D_INNER, LANES))
    final_w = final_norm_w[None, :]

    gw = HEADS_PER_GROUP * HEAD_DIM
    new_state = jax.ShapeDtypeStruct((batch, n_ssd, 2, N_GROUPS, gw, D_STATE), F32)
    h0_all = state_ssd.reshape(dec_batch, n_ssd, 2, N_GROUPS, gw, D_STATE)

    streams = [
        dict(first_row=0, tokens_per_cond=batch * seq, seqs_per_cond=batch, seg=seq, seq_len=seq,
             scan_seqs=SCAN_SEQS if batch % SCAN_SEQS == 0 else 1,
             fnet_seqs=FNET_SEQS if batch % FNET_SEQS == 0 else 1),
        dict(first_row=1, tokens_per_cond=dec_seq, seqs_per_cond=1, seg=GRID_W, seq_len=dec_seq, scan_seqs=1,
             fnet_seqs=1),
    ]
    acts = [x_prompt.reshape(batch * seq, D_MODEL), x_sample.reshape(dec_batch * dec_seq, D_MODEL)]
    for i in range(DEPTH):
        j = i // 2
        for s, st in enumerate(streams):
            x = acts[s]
            fr, tpc = st["first_row"], st["tokens_per_cond"]
            if i % 2 == 0:
                z, xc, bm, cm, dt = _ssd_in(x, mods, i, j, fr, tpc, st["seg"], norm_mix, w_in_t,
                                            ssd_conv_w, conv_b, dtb)
                if s == 0:
                    y, new_state = _ssd_scan(xc, bm, cm, dt, alog, dskip, st["seq_len"], st["scan_seqs"], j,
                                             state_out=new_state)
                else:
                    y, _ = _ssd_scan(xc, bm, cm, dt, alog, dskip, st["seq_len"], st["scan_seqs"], j, h0=h0_all)
                x, hb = _ssd_out(y, z, x, mods, i, j, fr, tpc, ssd_nw, ssd_w_out, norm_mlp)
            else:
                x, hb = _fnet(x, mods, i, j, fr, st["seqs_per_cond"], st["seq_len"], st["fnet_seqs"], norm_mix,
                              fno_w_out, fno_b, norm_mlp)
            acts[s] = _mlp(x, hb, mods, i, fr, tpc, mlp_w1, mlp_w2, final_w, i == DEPTH - 1)

    y_prompt = acts[0].reshape(batch, seq, D_MODEL)
    y_sample = acts[1].reshape(dec_batch, dec_seq, D_MODEL)
    new_state_ssd = new_state.reshape(batch, n_ssd, 2, N_HEADS, HEAD_DIM, D_STATE)
    return (y_prompt, y_sample, new_state_ssd)
```

```python
import functools

import numpy as np
import jax
import jax.numpy as jnp
from jax import lax
from jax.experimental import pallas as pl
from jax.experimental.pallas import tpu as pltpu

D_MODEL = 1024
DEPTH = 4
GRID_W = 64
D_INNER = 2048
HEAD_DIM = 64
N_HEADS = 32
N_GROUPS = 4
HEADS_PER_GROUP = N_HEADS // N_GROUPS
HG_SHIFT = HEADS_PER_GROUP.bit_length() - 1
D_STATE = 128
CHUNK = 128
D_BC = N_GROUPS * D_STATE
D_XBC = D_INNER + 2 * D_BC
D_ZX = D_INNER + D_XBC
D_IN_PROJ = D_ZX + 2 * N_HEADS
N_FGROUPS = 4
D_FG = D_MODEL // N_FGROUPS
D_FF = 4 * D_MODEL
EPS = 1e-6
N_COND = 8
LOG2E = 1.4426950408889634

LANES = 128
SSD_IN_TILE = 256
SSD_OUT_TILE = 512
SSD_OUT_ROWS = 512
MLP_TILE = 1024
SCAN_SEQS = 4
FNET_SEQS = 4
SHORT_SCAN_CHUNKS = 8
XBC_TILE = 512
FF_TILE = 1024
ADA_TILE = 1536
VMEM_LIMIT = 56 * 1024 * 1024

F32 = jnp.float32
BF16 = jnp.bfloat16


def _dot(a, b):
    return jnp.dot(a, b, preferred_element_type=F32)


def _dot_nt(a, b):
    return lax.dot_general(a, b, (((1,), (1,)), ((), ())), preferred_element_type=F32)


def _silu(x):
    h = 0.5 * x
    return h + h * jnp.tanh(h)


def _rms(x):
    return x * lax.rsqrt(jnp.mean(x * x, axis=-1, keepdims=True) + EPS)


def _const_spec(shape):
    nd = len(shape)
    return pl.BlockSpec(shape, lambda *_: (0,) * nd, pipeline_mode=pl.Buffered(1))


def _layer_spec(layer, tail):
    nd = len(tail)
    return pl.BlockSpec((None,) + tuple(tail), lambda *_: (layer,) + (0,) * nd,
                        pipeline_mode=pl.Buffered(1))


def _mod_spec(layer, tiles_per_cond, first_row):
    return pl.BlockSpec((None, None, 6, D_MODEL),
                        lambda t, *_: (layer, first_row + t // tiles_per_cond, 0, 0))


def _params(*semantics):
    return pltpu.CompilerParams(dimension_semantics=semantics, vmem_limit_bytes=VMEM_LIMIT)


def _cast_weight(dst_ref, src_ref, n_cols, tile=512):
    for j in range(0, n_cols, tile):
        dst_ref[:, j:j + tile] = src_ref[:, j:j + tile].astype(BF16)


def _ada_kernel(c_ref, w_ref, b_ref, o_ref):
    c = c_ref[...]
    s = _silu(c).astype(BF16)
    o_ref[...] = _dot(s, w_ref[...].astype(BF16)) + b_ref[...]


def _ada_modulation(cond, ada_w, ada_b):
    n_out = 6 * D_MODEL
    out = pl.pallas_call(
        _ada_kernel,
        grid=(DEPTH, n_out // ADA_TILE),
        in_specs=[
            pl.BlockSpec((N_COND, D_MODEL), lambda i, n: (0, 0)),
            pl.BlockSpec((None, D_MODEL, ADA_TILE), lambda i, n: (i, 0, n)),
            pl.BlockSpec((None, 1, ADA_TILE), lambda i, n: (i, 0, n)),
        ],
        out_specs=pl.BlockSpec((None, N_COND, ADA_TILE), lambda i, n: (i, 0, n)),
        out_shape=jax.ShapeDtypeStruct((DEPTH, N_COND, n_out), F32),
        compiler_params=_params("parallel", "parallel"),
        name="ada_mod",
    )(cond, ada_w, ada_b.reshape(DEPTH, 1, n_out))
    return out.reshape(DEPTH, N_COND, 6, D_MODEL)


def _ssd_in_kernel(x_ref, mod_ref, nw_ref, win_ref, cw_ref, cb_ref, dtb_ref,
                   z_ref, xs_ref, b_ref, c_ref, dt_ref, wbf_ref, wdt_ref, *, seg):
    @pl.when(pl.program_id(0) == 0)
    def _():
        for r0 in range(0, D_ZX, XBC_TILE):
            wbf_ref[r0:r0 + XBC_TILE, :] = win_ref[r0:r0 + XBC_TILE, :].astype(BF16)
        n = lax.broadcasted_iota(jnp.int32, (2 * N_HEADS, 2 * N_HEADS), 0)
        k = lax.broadcasted_iota(jnp.int32, (2 * N_HEADS, 2 * N_HEADS), 1)
        src = (((n >> HG_SHIFT) & 1) * N_HEADS + (n >> (HG_SHIFT + 1)) * HEADS_PER_GROUP
               + (n & (HEADS_PER_GROUP - 1)))
        pick = jnp.where(k == src, 1.0, 0.0).astype(BF16)
        wdt_ref[...] = _dot(pick, win_ref[D_ZX:D_IN_PROJ, :].astype(BF16)).astype(BF16)

    tm = x_ref.shape[0]
    h = _rms(x_ref[...]) * nw_ref[...]
    h = h * (1.0 + mod_ref[1:2, :]) + mod_ref[0:1, :]
    hb = h.astype(BF16)
    for j in range(D_INNER // XBC_TILE):
        cols = slice(j * XBC_TILE, (j + 1) * XBC_TILE)
        z_ref[:, cols] = _dot_nt(hb, wbf_ref[cols, :]).astype(BF16)

    row = lax.broadcasted_iota(jnp.int32, (tm, XBC_TILE), 0) & (seg - 1)
    first = row == 0
    last = row == seg - 1
    for j in range(D_XBC // XBC_TILE):
        cols = slice(j * XBC_TILE, (j + 1) * XBC_TILE)
        acc = _dot_nt(hb, wbf_ref[D_INNER + j * XBC_TILE:D_INNER + (j + 1) * XBC_TILE, :])
        up = jnp.where(first, 0.0, pltpu.roll(acc, 1, axis=0))
        dn = jnp.where(last, 0.0, pltpu.roll(acc, tm - 1, axis=0))
        o = up * cw_ref[0:1, cols] + acc * cw_ref[1:2, cols] + dn * cw_ref[2:3, cols] + cb_ref[:, cols]
        o = _silu(o.astype(BF16))
        if j < D_INNER // XBC_TILE:
            xs_ref[:, cols] = o
        elif j == D_INNER // XBC_TILE:
            b_ref[...] = o
        else:
            c_ref[...] = o

    dtT = _dot_nt(wdt_ref[...], hb)
    for k in range(tm // CHUNK):
        v = dtT[:, k * CHUNK:(k + 1) * CHUNK] + dtb_ref[...]
        dt_ref[k] = jnp.maximum(v, 0.0) + jnp.log1p(jnp.exp(-jnp.abs(v)))


def _ssd_in(x, mods, layer, j, first_row, tokens_per_cond, seg, norm_w, w_in, conv_w, conv_b, dtb):
    t_tokens = x.shape[0]
    tm = SSD_IN_TILE
    tok = lambda n: pl.BlockSpec((tm, n), lambda t: (t, 0))
    return pl.pallas_call(
        functools.partial(_ssd_in_kernel, seg=seg),
        grid=(t_tokens // tm,),
        in_specs=[
            tok(D_MODEL),
            _mod_spec(layer, tokens_per_cond // tm, first_row),
            _layer_spec(layer, (1, D_MODEL)),
            _layer_spec(j, (D_IN_PROJ, D_MODEL)),
            _layer_spec(j, (3, D_XBC)),
            _layer_spec(j, (1, D_XBC)),
            _layer_spec(j, (2 * N_HEADS, LANES)),
        ],
        out_specs=[
            tok(D_INNER), tok(D_INNER), tok(D_BC), tok(D_BC),
            pl.BlockSpec((tm // CHUNK, 2 * N_HEADS, LANES), lambda t: (t, 0, 0)),
        ],
        out_shape=[
            jax.ShapeDtypeStruct((t_tokens, D_INNER), BF16),
            jax.ShapeDtypeStruct((t_tokens, D_INNER), BF16),
            jax.ShapeDtypeStruct((t_tokens, D_BC), BF16),
            jax.ShapeDtypeStruct((t_tokens, D_BC), BF16),
            jax.ShapeDtypeStruct((t_tokens // CHUNK, 2 * N_HEADS, LANES), F32),
        ],
        scratch_shapes=[pltpu.VMEM((D_ZX, D_MODEL), BF16), pltpu.VMEM((2 * N_HEADS, D_MODEL), BF16)],
        compiler_params=_params("arbitrary"),
        name="ssd_in",
    )(x, mods, norm_w, w_in, conv_w, conv_b, dtb)


def _split3(a):
    hi = a.astype(BF16)
    r = a - hi.astype(F32)
    mid = r.astype(BF16)
    lo = (r - mid.astype(F32)).astype(BF16)
    return hi, mid, lo


def _dot3(a, m):
    hi, mid, lo = _split3(a)
    return _dot(hi, m) + _dot(mid, m) + _dot(lo, m)


def _ssd_scan_kernel(*refs, seq_len, seqs, has_h0, emit_state, has_alias):
    xs_ref, b_ref, c_ref, dt_ref, alog_ref, dskip_ref = refs[:6]
    pos = 6
    h0_ref = refs[pos] if has_h0 else None
    pos += int(has_h0) + int(has_alias)
    y_ref = refs[pos]
    hf_ref = refs[pos + 1] if emit_state else None
    ht_ref = refs[-1]

    nc = seq_len // CHUNK
    hg = HEADS_PER_GROUP
    rows = lax.broadcasted_iota(jnp.int32, (CHUNK, CHUNK), 0)
    lanes = lax.broadcasted_iota(jnp.int32, (CHUNK, CHUNK), 1)
    tril = rows >= lanes
    triu = rows <= lanes
    tril_b = jnp.where(tril, 1.0, 0.0).astype(BF16)
    triu_b = jnp.where(triu, 1.0, 0.0).astype(BF16)
    ones_b = jnp.ones((CHUNK, CHUNK), BF16)
    low_half = lanes < HEAD_DIM
    low_b = jnp.where(low_half, 1.0, 0.0).astype(BF16)
    high_b = jnp.where(low_half, 0.0, 1.0).astype(BF16)
    a2 = -jnp.exp(alog_ref[...]) * LOG2E

    sum_tot_b = [jnp.concatenate([triu_b, ones_b], axis=1), jnp.concatenate([tril_b, ones_b], axis=1)]

    def scan_short(s):
        below = rows > lanes
        above = rows < lanes
        y, st, cols, cdp, c16 = {}, {}, {}, {}, {}
        for c in range(nc):
            t0 = s * seq_len + c * CHUNK
            dt16 = dt_ref[s * nc + c]
            da16 = dt16 * a2
            parts = jnp.concatenate(_split3(da16), axis=0)
            dts, das, cs_t, w_t, cd, src_t = [], [], [], [], [], []
            for d in range(2):
                res = _dot(parts, sum_tot_b[d])
                sums = (res[0:2 * hg] + res[2 * hg:4 * hg] + res[4 * hg:6 * hg])[d * hg:(d + 1) * hg]
                dts.append(dt16[d * hg:(d + 1) * hg])
                das.append(da16[d * hg:(d + 1) * hg])
                cs_t.append(sums[:, 0:CHUNK])
                tot = sums[:, CHUNK:2 * CHUNK]
                w_t.append(dts[d] * jnp.exp2(tot - cs_t[d]))
                cd.append(jnp.exp2(tot))
                src_t.append(cs_t[d] - jnp.log(dts[d]) * LOG2E)
            diag_t = jnp.log(dts[0] + dts[1]) * LOG2E
            bc = b_ref[t0:t0 + CHUNK, :]
            c16[c] = c_ref[t0:t0 + CHUNK, :]
            cb = _dot_nt(c16[c], bc)
            b_t = bc.astype(F32).T.astype(BF16)
            w16 = [w.astype(BF16) for w in w_t]
            for q in range(hg // 2):
                sl = slice(q * LANES, (q + 1) * LANES)
                xpb = xs_ref[t0:t0 + CHUNK, sl]
                x2 = jnp.concatenate([xpb * low_b, xpb * high_b], axis=0)
                m_parts, bw_parts, col_parts = [], ([], []), ([], [])
                for r in (2 * q, 2 * q + 1):
                    col_f = jnp.sum(jnp.where(tril, das[0][r:r + 1, :], 0.0), axis=1, keepdims=True)
                    col_b = jnp.sum(jnp.where(triu, das[1][r:r + 1, :], 0.0), axis=1, keepdims=True)
                    arg = jnp.where(below, col_f - src_t[0][r:r + 1, :],
                                    jnp.where(above, col_b - src_t[1][r:r + 1, :], diag_t[r:r + 1, :]))
                    m_parts.append((cb * jnp.exp2(arg)).astype(BF16))
                    for d, col in ((0, col_f), (1, col_b)):
                        bw_parts[d].append(b_t * w16[d][r:r + 1, :])
                        col_parts[d].append(col)
                y[c, q] = _dot(jnp.concatenate(m_parts, axis=1), x2) + xpb.astype(F32) * dskip_ref[:, sl]
                for d in range(2):
                    st[d, c, q] = _dot(jnp.concatenate(bw_parts[d], axis=1), x2)
                    cols[d, c, q] = col_parts[d]
                    cdp[d, c, q] = jnp.where(low_half, cd[d][2 * q:2 * q + 1, :], cd[d][2 * q + 1:2 * q + 2, :])

        for d in range(2):
            order = range(nc) if d == 0 else range(nc - 1, -1, -1)
            for q in range(hg // 2):
                sl = slice(q * LANES, (q + 1) * LANES)
                h = h0_ref[s, d, sl, :].T if has_h0 else None
                for c in order:
                    if h is None:
                        h = st[d, c, q]
                    else:
                        ecs = jnp.exp2(jnp.where(low_half, cols[d, c, q][0], cols[d, c, q][1]))
                        y[c, q] = y[c, q] + _dot(c16[c], h.astype(BF16)) * ecs
                        h = h * cdp[d, c, q] + st[d, c, q]
                if emit_state:
                    hf_ref[s, d, sl, :] = h.T
        for c in range(nc):
            for q in range(hg // 2):
                t0 = s * seq_len + c * CHUNK
                y_ref[t0:t0 + CHUNK, q * LANES:(q + 1) * LANES] = y[c, q].astype(BF16)

    if nc <= SHORT_SCAN_CHUNKS:
        for s in range(seqs):
            scan_short(s)
        return

    y_ref[...] = jnp.zeros(y_ref.shape, y_ref.dtype)
    for s in range(seqs):
        for d in range(2):
            for q in range(hg // 2):
                sl = slice(q * LANES, (q + 1) * LANES)
                if has_h0:
                    ht_ref[s, d, :, sl] = h0_ref[s, d, sl, :].T
                else:
                    ht_ref[s, d, :, sl] = jnp.zeros((D_STATE, LANES), F32)

    def chunk_step(s, d, c):
        mask = tril if d == 0 else triu
        t0 = s * seq_len + pl.multiple_of(c * CHUNK, CHUNK)
        dt16 = dt_ref[s * nc + c]
        da16 = dt16 * a2
        res = _dot(jnp.concatenate(_split3(da16), axis=0), sum_tot_b[d])
        sums = res[0:2 * hg] + res[2 * hg:4 * hg] + res[4 * hg:6 * hg]
        cs_t = sums[d * hg:(d + 1) * hg, 0:CHUNK]
        tot = sums[d * hg:(d + 1) * hg, CHUNK:2 * CHUNK]
        dt8 = dt16[d * hg:(d + 1) * hg]
        da8 = da16[d * hg:(d + 1) * hg]
        w_t = dt8 * jnp.exp2(tot - cs_t)
        cd = jnp.exp2(tot)
        src_t = cs_t - jnp.log(dt8) * LOG2E
        bc = b_ref[pl.ds(t0, CHUNK), :]
        cb16 = c_ref[pl.ds(t0, CHUNK), :]
        cb = _dot_nt(cb16, bc)
        b_t = bc.astype(F32).T

        for q in range(hg // 2):
            sl = slice(q * LANES, (q + 1) * LANES)
            xpb = xs_ref[pl.ds(t0, CHUNK), sl]
            x2 = jnp.concatenate([xpb * low_b, xpb * high_b], axis=0)
            m_parts, bw_parts, c_parts = [], [], []
            for r in (2 * q, 2 * q + 1):
                cs_col = jnp.sum(jnp.where(mask, da8[r:r + 1, :], 0.0), axis=1, keepdims=True)
                decay = jnp.where(mask, jnp.exp2(cs_col - src_t[r:r + 1, :]), 0.0)
                m_parts.append((cb * decay).astype(BF16))
                bw_parts.append((b_t * w_t[r:r + 1, :]).astype(BF16))
                c_parts.append(cs_col)
            y = _dot(jnp.concatenate(m_parts, axis=1), x2) + y_ref[pl.ds(t0, CHUNK), sl].astype(F32)
            st = _dot(jnp.concatenate(bw_parts, axis=1), x2)
            htp = ht_ref[s, d, :, sl]
            ecs = jnp.exp2(jnp.where(low_half, c_parts[0], c_parts[1]))
            cdp = jnp.where(low_half, cd[2 * q:2 * q + 1, :], cd[2 * q + 1:2 * q + 2, :])
            y = y + _dot(cb16, htp.astype(BF16)) * ecs
            ht_ref[s, d, :, sl] = htp * cdp + st
            if d == 0:
                y = y + xpb.astype(F32) * dskip_ref[:, sl]
            y_ref[pl.ds(t0, CHUNK), sl] = y.astype(BF16)

    def body(i, carry):
        for s in range(seqs):
            chunk_step(s, 0, i)
            chunk_step(s, 1, nc - 1 - i)
        return carry

    lax.fori_loop(0, nc, body, 0, unroll=2)

    if emit_state:
        for s in range(seqs):
            for d in range(2):
                for q in range(hg // 2):
                    sl = slice(q * LANES, (q + 1) * LANES)
                    hf_ref[s, d, sl, :] = ht_ref[s, d, :, sl].T


def _ssd_scan(xs, bm, cm, dt, alog, dskip, seq_len, seqs, ssd_index, h0=None, state_out=None):
    t_tokens = xs.shape[0]
    rows = seq_len * seqs
    nb = t_tokens // rows
    gw = HEADS_PER_GROUP * HEAD_DIM
    nc = seq_len // CHUNK
    in_specs = [
        pl.BlockSpec((rows, gw), lambda b, g: (b, g)),
        pl.BlockSpec((rows, D_STATE), lambda b, g: (b, g)),
        pl.BlockSpec((rows, D_STATE), lambda b, g: (b, g)),
        pl.BlockSpec((seqs * nc, 2 * HEADS_PER_GROUP, LANES), lambda b, g: (b, g, 0)),
        pl.BlockSpec((None, 2 * HEADS_PER_GROUP, LANES), lambda b, g: (ssd_index, g, 0)),
        pl.BlockSpec((None, 1, gw), lambda b, g: (ssd_index, 0, g)),
    ]
    args = [xs, bm, cm, dt, alog, dskip]
    has_h0 = h0 is not None
    if has_h0:
        in_specs.append(pl.BlockSpec((seqs, None, 2, None, gw, D_STATE),
                                     lambda b, g: (b, ssd_index, 0, g, 0, 0)))
        args.append(h0)
    emit_state = state_out is not None
    has_alias = emit_state and not isinstance(state_out, jax.ShapeDtypeStruct)
    out_specs = [pl.BlockSpec((rows, gw), lambda b, g: (b, g))]
    out_shape = [jax.ShapeDtypeStruct((t_tokens, D_INNER), BF16)]
    aliases = {}
    if emit_state:
        if has_alias:
            in_specs.append(pl.BlockSpec(memory_space=pl.ANY))
            args.append(state_out)
            aliases = {len(args) - 1: 1}
        state_sds = jax.ShapeDtypeStruct(state_out.shape, state_out.dtype)
        out_specs.append(pl.BlockSpec((seqs, None, 2, None, gw, D_STATE),
                                      lambda b, g: (b, ssd_index, 0, g, 0, 0)))
        out_shape.append(state_sds)
    res = pl.pallas_call(
        functools.partial(_ssd_scan_kernel, seq_len=seq_len, seqs=seqs, has_h0=has_h0,
                          emit_state=emit_state, has_alias=has_alias),
        grid=(nb, N_GROUPS),
        in_specs=in_specs,
        out_specs=out_specs,
        out_shape=out_shape,
        scratch_shapes=[pltpu.VMEM((seqs, 2, D_STATE, gw), F32)],
        input_output_aliases=aliases,
        compiler_params=_params("parallel", "parallel"),
        name="ssd_scan",
    )(*args)
    return res if emit_state else (res[0], None)


def _mlp_input(x, mod_ref, nw_ref):
    h = _rms(x) * nw_ref[...]
    return (h * (1.0 + mod_ref[4:5, :]) + mod_ref[3:4, :]).astype(BF16)


def _ssd_out_kernel(y_ref, z_ref, x_ref, mod_ref, nw_ref, wo_ref, nwm_ref, o_ref, hb_ref, wbf_ref):
    @pl.when(pl.program_id(0) == 0)
    def _():
        nw_cols = jnp.concatenate([nw_ref[...]] * (XBC_TILE // LANES), axis=1)
        for j in range(0, D_MODEL, XBC_TILE):
            wbf_ref[:, j:j + XBC_TILE] = (wo_ref[:, j:j + XBC_TILE] * nw_cols).astype(BF16)

    for r0 in range(0, x_ref.shape[0], SSD_OUT_ROWS):
        rs = slice(r0, r0 + SSD_OUT_ROWS)
        g = y_ref[rs, :] * _silu(z_ref[rs, :])
        gf = g.astype(F32)
        scale = lax.rsqrt(jnp.mean(gf * gf, axis=-1, keepdims=True) + EPS)
        x1 = x_ref[rs, :] + mod_ref[2:3, :] * (_dot(g, wbf_ref[...]) * scale)
        o_ref[rs, :] = x1
        hb_ref[rs, :] = _mlp_input(x1, mod_ref, nwm_ref)


def _ssd_out(y, z, x, mods, layer, j, first_row, tokens_per_cond, norm_w, w_out, norm_mlp):
    t_tokens = x.shape[0]
    tm = SSD_OUT_TILE
    tok = lambda n: pl.BlockSpec((tm, n), lambda t: (t, 0))
    return pl.pallas_call(
        _ssd_out_kernel,
        grid=(t_tokens // tm,),
        in_specs=[tok(D_INNER), tok(D_INNER), tok(D_MODEL),
                  _mod_spec(layer, tokens_per_cond // tm, first_row),
                  _layer_spec(j, (D_INNER, LANES)), _layer_spec(j, (D_INNER, D_MODEL)),
                  _layer_spec(layer, (1, D_MODEL))],
        out_specs=[tok(D_MODEL), tok(D_MODEL)],
        out_shape=[jax.ShapeDtypeStruct((t_tokens, D_MODEL), F32),
                   jax.ShapeDtypeStruct((t_tokens, D_MODEL), BF16)],
        scratch_shapes=[pltpu.VMEM((D_INNER, D_MODEL), BF16)],
        compiler_params=_params("arbitrary"),
        name="ssd_out",
    )(y, z, x, mods, norm_w, w_out, norm_mlp)


def _mlp_kernel(x_ref, hb_ref, mod_ref, w1_ref, w2_ref, fw_ref, o_ref, acc_ref, *, final):
    k = pl.program_id(1)
    last = pl.num_programs(1) - 1

    def block():
        a = jnp.maximum(_dot(hb_ref[...], w1_ref[...].astype(BF16)), 0.0)
        return _dot((a * a).astype(BF16), w2_ref[...].astype(BF16))

    @pl.when(k == 0)
    def _():
        acc_ref[...] = block()

    @pl.when(jnp.logical_and(k > 0, k < last))
    def _():
        acc_ref[...] += block()

    @pl.when(k == last)
    def _():
        x2 = x_ref[...] + mod_ref[5:6, :] * (acc_ref[...] + block())
        if final:
            x2 = _rms(x2) * fw_ref[...]
        o_ref[...] = x2


def _mlp(x, hb, mods, layer, first_row, tokens_per_cond, w1, w2, fw, final):
    t_tokens = x.shape[0]
    tm = MLP_TILE
    assert D_FF // FF_TILE >= 2
    tok = pl.BlockSpec((tm, D_MODEL), lambda t, k: (t, 0))
    return pl.pallas_call(
        functools.partial(_mlp_kernel, final=final),
        grid=(t_tokens // tm, D_FF // FF_TILE),
        in_specs=[tok, tok, _mod_spec(layer, tokens_per_cond // tm, first_row),
                  pl.BlockSpec((None, D_MODEL, FF_TILE), lambda t, k: (layer, 0, k)),
                  pl.BlockSpec((None, FF_TILE, D_MODEL), lambda t, k: (layer, k, 0)),
                  _const_spec((1, D_MODEL))],
        out_specs=tok,
        out_shape=jax.ShapeDtypeStruct((t_tokens, D_MODEL), F32),
        scratch_shapes=[pltpu.VMEM((tm, D_MODEL), F32)],
        compiler_params=_params("parallel", "arbitrary"),
        name="mlp",
    )(x, hb, mods, w1, w2, fw)


@functools.lru_cache(maxsize=None)
def _dft_matrices(n):
    k = np.arange(n)
    ang = 2.0 * np.pi * ((k[:, None] * k[None, :]) % n) / n
    return np.cos(ang) / np.sqrt(n), np.sin(ang) / np.sqrt(n)


def _fnet_kernel(x_ref, mod_ref, nw_ref, chan_ref, seqm_ref, wf_ref, bf_ref, nwm_ref, o_ref, hb_ref,
                 f_ref, wbf_ref, *, seq_len):
    @pl.when(pl.program_id(0) == 0)
    def _():
        _cast_weight(wbf_ref, wf_ref, D_MODEL)

    x = x_ref[...]
    h = _rms(x) * nw_ref[...]
    hb = (h * (1.0 + mod_ref[1:2, :]) + mod_ref[0:1, :]).astype(BF16)
    for g in range(N_FGROUPS):
        cols = slice(g * D_FG, (g + 1) * D_FG)
        p = _dot(hb[:, cols], chan_ref[...]).astype(BF16)
        for r0 in range(0, x_ref.shape[0], seq_len):
            rs = slice(r0, r0 + seq_len)
            stacked = jnp.concatenate([p[rs, :D_FG], p[rs, D_FG:]], axis=0)
            f_ref[rs, cols] = _dot(seqm_ref[...], stacked).astype(BF16)
    x1 = x + mod_ref[2:3, :] * (_dot(f_ref[...], wbf_ref[...]) + bf_ref[...])
    o_ref[...] = x1
    hb_ref[...] = _mlp_input(x1, mod_ref, nwm_ref)


def _fnet(x, mods, layer, j, first_row, seqs_per_cond, seq_len, seqs, norm_w, w_f, b_f, norm_mlp):
    t_tokens = x.shape[0]
    rows = seq_len * seqs
    cc, sc = _dft_matrices(D_FG)
    cl, sl = _dft_matrices(seq_len)
    chan = jnp.asarray(np.concatenate([cc, sc], axis=1), F32).astype(BF16)
    seqm = jnp.asarray(np.concatenate([cl, -sl], axis=1), F32).astype(BF16)
    tok = pl.BlockSpec((rows, D_MODEL), lambda t: (t, 0))
    return pl.pallas_call(
        functools.partial(_fnet_kernel, seq_len=seq_len),
        grid=(t_tokens // rows,),
        in_specs=[tok, _mod_spec(layer, seqs_per_cond // seqs, first_row), _layer_spec(layer, (1, D_MODEL)),
                  _const_spec((D_FG, 2 * D_FG)), _const_spec((seq_len, 2 * seq_len)),
                  _layer_spec(j, (D_MODEL, D_MODEL)), _layer_spec(j, (1, D_MODEL)),
                  _layer_spec(layer, (1, D_MODEL))],
        out_specs=[tok, tok],
        out_shape=[jax.ShapeDtypeStruct((t_tokens, D_MODEL), F32),
                   jax.ShapeDtypeStruct((t_tokens, D_MODEL), BF16)],
        scratch_shapes=[pltpu.VMEM((rows, D_MODEL), BF16), pltpu.VMEM((D_MODEL, D_MODEL), BF16)],
        compiler_params=_params("arbitrary"),
        name="fnet",
    )(x, mods, norm_w, chan, seqm, w_f, b_f, norm_mlp)


def _head_perm():
    return np.array([d * N_HEADS + g * HEADS_PER_GROUP + r
                     for g in range(N_GROUPS) for d in range(2) for r in range(HEADS_PER_GROUP)])


def kernel(x_prompt, x_sample, state_ssd, c, c_ctx, ada_w, ada_b, norm_mix_w, norm_mlp_w, ssd_w_in, ssd_conv_w, ssd_conv_b, ssd_dt_bias, ssd_a_log, ssd_d, ssd_norm_w, ssd_w_out, fno_w_out, fno_b_out, mlp_w1, mlp_w2, final_norm_w):
    batch, seq, _ = x_prompt.shape
    dec_batch, dec_seq, _ = x_sample.shape
    n_ssd = ssd_w_in.shape[0]

    cond = jnp.concatenate([c_ctx[None], c, jnp.zeros((N_COND - 1 - dec_batch, D_MODEL), F32)], axis=0)
    mods = _ada_modulation(cond, ada_w, ada_b)

    perm = _head_perm()
    lane_rep = lambda v: jnp.broadcast_to(v.reshape(n_ssd, 2 * N_HEADS)[:, perm][:, :, None],
                                          (n_ssd, 2 * N_HEADS, LANES))
    dtb = lane_rep(ssd_dt_bias)
    alog = lane_rep(ssd_a_log)
    dskip = jnp.repeat(ssd_d, HEAD_DIM, axis=1)[:, None, :]
    rows3 = lambda v: v[:, None, :]
    w_in_t = jnp.swapaxes(ssd_w_in, 1, 2)
    norm_mix, norm_mlp = rows3(norm_mix_w), rows3(norm_mlp_w)
    conv_b, fno_b = rows3(ssd_conv_b), rows3(fno_b_out)
    ssd_nw = jnp.broadcast_to(ssd_norm_w[:, :, None], (n_ssd, D_INNER, LANES))
    final_w = final_norm_w[None, :]

    gw = HEADS_PER_GROUP * HEAD_DIM
    new_state = jax.ShapeDtypeStruct((batch, n_ssd, 2, N_GROUPS, gw, D_STATE), F32)
    h0_all = state_ssd.reshape(dec_batch, n_ssd, 2, N_GROUPS, gw, D_STATE)

    streams = [
        dict(first_row=0, tokens_per_cond=batch * seq, seqs_per_cond=batch, seg=seq, seq_len=seq,
             scan_seqs=SCAN_SEQS if batch % SCAN_SEQS == 0 else 1,
             fnet_seqs=FNET_SEQS if batch % FNET_SEQS == 0 else 1),
        dict(first_row=1, tokens_per_cond=dec_seq, seqs_per_cond=1, seg=GRID_W, seq_len=dec_seq, scan_seqs=1,
             fnet_seqs=1),
    ]
    acts = [x_prompt.reshape(batch * seq, D_MODEL), x_sample.reshape(dec_batch * dec_seq, D_MODEL)]
    for i in range(DEPTH):
        j = i // 2
        for s, st in enumerate(streams):
            x = acts[s]
            fr, tpc = st["first_row"], st["tokens_per_cond"]
            if i % 2 == 0:
                z, xc, bm, cm, dt = _ssd_in(x, mods, i, j, fr, tpc, st["seg"], norm_mix, w_in_t,
                                            ssd_conv_w, conv_b, dtb)
                if s == 0:
                    y, new_state = _ssd_scan(xc, bm, cm, dt, alog, dskip, st["seq_len"], st["scan_seqs"], j,
                                             state_out=new_state)
                else:
                    y, _ = _ssd_scan(xc, bm, cm, dt, alog, dskip, st["seq_len"], st["scan_seqs"], j, h0=h0_all)
                x, hb = _ssd_out(y, z, x, mods, i, j, fr, tpc, ssd_nw, ssd_w_out, norm_mlp)
            else:
                x, hb = _fnet(x, mods, i, j, fr, st["seqs_per_cond"], st["seq_len"], st["fnet_seqs"], norm_mix,
                              fno_w_out, fno_b, norm_mlp)
            acts[s] = _mlp(x, hb, mods, i, fr, tpc, mlp_w1, mlp_w2, final_w, i == DEPTH - 1)

    y_prompt = acts[0].reshape(batch, seq, D_MODEL)
    y_sample = acts[1].reshape(dec_batch, dec_seq, D_MODEL)
    new_state_ssd = new_state.reshape(batch, n_ssd, 2, N_HEADS, HEAD_DIM, D_STATE)
    return (y_prompt, y_sample, new_state_ssd)
```

```python
import functools

import numpy as np
import jax
import jax.numpy as jnp
from jax import lax
from jax.experimental import pallas as pl
from jax.experimental.pallas import tpu as pltpu

D_MODEL = 1024
DEPTH = 4
GRID_W = 64
D_INNER = 2048
HEAD_DIM = 64
N_HEADS = 32
N_GROUPS = 4
HEADS_PER_GROUP = N_HEADS // N_GROUPS
HG_SHIFT = HEADS_PER_GROUP.bit_length() - 1
D_STATE = 128
CHUNK = 128
D_BC = N_GROUPS * D_STATE
D_XBC = D_INNER + 2 * D_BC
D_ZX = D_INNER + D_XBC
D_IN_PROJ = D_ZX + 2 * N_HEADS
N_FGROUPS = 4
D_FG = D_MODEL // N_FGROUPS
D_FF = 4 * D_MODEL
EPS = 1e-6
N_COND = 8
LOG2E = 1.4426950408889634

LANES = 128
SUBLANES = 8
SSD_IN_TILE = 256
SSD_OUT_TILE = 512
SSD_OUT_ROWS = 512
MLP_TILE = 1024
SCAN_SEQS = 4
FNET_SEQS = 4
SHORT_SCAN_CHUNKS = 8
XBC_TILE = 512
FF_TILE = 1024
ADA_TILE = 1536
VMEM_LIMIT = 56 * 1024 * 1024

F32 = jnp.float32
BF16 = jnp.bfloat16


def _dot(a, b):
    return jnp.dot(a, b, preferred_element_type=F32)


def _dot_nt(a, b):
    return lax.dot_general(a, b, (((1,), (1,)), ((), ())), preferred_element_type=F32)


def _silu(x):
    h = 0.5 * x
    return h + h * jnp.tanh(h)


def _rms(x):
    return x * lax.rsqrt(jnp.mean(x * x, axis=-1, keepdims=True) + EPS)


def _const_spec(shape):
    nd = len(shape)
    return pl.BlockSpec(shape, lambda *_: (0,) * nd, pipeline_mode=pl.Buffered(1))


def _layer_spec(layer, tail):
    nd = len(tail)
    return pl.BlockSpec((None,) + tuple(tail), lambda *_: (layer,) + (0,) * nd,
                        pipeline_mode=pl.Buffered(1))


def _mod_spec(layer, tiles_per_cond, first_row):
    return pl.BlockSpec((None, None, 6, D_MODEL),
                        lambda t, *_: (layer, first_row + t // tiles_per_cond, 0, 0))


def _params(*semantics):
    return pltpu.CompilerParams(dimension_semantics=semantics, vmem_limit_bytes=VMEM_LIMIT)


def _cast_weight(dst_ref, src_ref, n_cols, tile=512):
    for j in range(0, n_cols, tile):
        dst_ref[:, j:j + tile] = src_ref[:, j:j + tile].astype(BF16)


def _ada_kernel(c_ref, w_ref, b_ref, o_ref):
    c = c_ref[...]
    s = _silu(c).astype(BF16)
    o_ref[...] = _dot(s, w_ref[...].astype(BF16)) + b_ref[...]


def _ada_modulation(cond, ada_w, ada_b):
    n_out = 6 * D_MODEL
    out = pl.pallas_call(
        _ada_kernel,
        grid=(DEPTH, n_out // ADA_TILE),
        in_specs=[
            pl.BlockSpec((N_COND, D_MODEL), lambda i, n: (0, 0)),
            pl.BlockSpec((None, D_MODEL, ADA_TILE), lambda i, n: (i, 0, n)),
            pl.BlockSpec((None, 1, ADA_TILE), lambda i, n: (i, 0, n)),
        ],
        out_specs=pl.BlockSpec((None, N_COND, ADA_TILE), lambda i, n: (i, 0, n)),
        out_shape=jax.ShapeDtypeStruct((DEPTH, N_COND, n_out), F32),
        compiler_params=_params("parallel", "parallel"),
        name="ada_mod",
    )(cond, ada_w, ada_b.reshape(DEPTH, 1, n_out))
    return out.reshape(DEPTH, N_COND, 6, D_MODEL)


def _ssd_in_kernel(x_ref, mod_ref, nw_ref, win_ref, cw_ref, cb_ref, dtb_ref,
                   z_ref, xs_ref, b_ref, c_ref, dt_ref, wbf_ref, wdt_ref, *, seg):
    @pl.when(pl.program_id(0) == 0)
    def _():
        for r0 in range(0, D_ZX, XBC_TILE):
            wbf_ref[r0:r0 + XBC_TILE, :] = win_ref[r0:r0 + XBC_TILE, :].astype(BF16)
        n = lax.broadcasted_iota(jnp.int32, (2 * N_HEADS, 2 * N_HEADS), 0)
        k = lax.broadcasted_iota(jnp.int32, (2 * N_HEADS, 2 * N_HEADS), 1)
        src = (((n >> HG_SHIFT) & 1) * N_HEADS + (n >> (HG_SHIFT + 1)) * HEADS_PER_GROUP
               + (n & (HEADS_PER_GROUP - 1)))
        pick = jnp.where(k == src, 1.0, 0.0).astype(BF16)
        wdt_ref[...] = _dot(pick, win_ref[D_ZX:D_IN_PROJ, :].astype(BF16)).astype(BF16)

    tm = x_ref.shape[0]
    h = _rms(x_ref[...]) * nw_ref[...]
    h = h * (1.0 + mod_ref[1:2, :]) + mod_ref[0:1, :]
    hb = h.astype(BF16)
    for j in range(D_INNER // XBC_TILE):
        cols = slice(j * XBC_TILE, (j + 1) * XBC_TILE)
        z_ref[:, cols] = _dot_nt(hb, wbf_ref[cols, :]).astype(BF16)

    sub_row = lax.broadcasted_iota(jnp.int32, (SUBLANES, XBC_TILE), 0)
    for j in range(D_XBC // XBC_TILE):
        cols = slice(j * XBC_TILE, (j + 1) * XBC_TILE)
        acc = _dot_nt(hb, wbf_ref[D_INNER + j * XBC_TILE:D_INNER + (j + 1) * XBC_TILE, :])
        up = pltpu.roll(acc, 1, axis=0)
        dn = pltpu.roll(acc, tm - 1, axis=0)
        w_up, w_dn = cw_ref[0:1, cols], cw_ref[2:3, cols]
        o = up * w_up + acc * cw_ref[1:2, cols] + dn * w_dn + cb_ref[:, cols]
        pieces, done = [], 0
        for lo in range(0, tm, SUBLANES):
            hi = lo + SUBLANES
            is_first, is_last = lo % seg == 0, hi % seg == 0
            if not (is_first or is_last):
                continue
            if lo > done:
                pieces.append(o[done:lo])
            blk = o[lo:hi]
            if is_first:
                blk = blk - jnp.where(sub_row == 0, up[lo:hi] * w_up, 0.0)
            if is_last:
                blk = blk - jnp.where(sub_row == SUBLANES - 1, dn[lo:hi] * w_dn, 0.0)
            pieces.append(blk)
            done = hi
        if done < tm:
            pieces.append(o[done:])
        o = jnp.concatenate(pieces, axis=0)
        o = _silu(o.astype(BF16))
        if j < D_INNER // XBC_TILE:
            xs_ref[:, cols] = o
        elif j == D_INNER // XBC_TILE:
            b_ref[...] = o
        else:
            c_ref[...] = o

    dtT = _dot_nt(wdt_ref[...], hb)
    for k in range(tm // CHUNK):
        v = dtT[:, k * CHUNK:(k + 1) * CHUNK] + dtb_ref[...]
        dt_ref[k] = jnp.maximum(v, 0.0) + jnp.log1p(jnp.exp(-jnp.abs(v)))


def _ssd_in(x, mods, layer, j, first_row, tokens_per_cond, seg, norm_w, w_in, conv_w, conv_b, dtb):
    t_tokens = x.shape[0]
    tm = SSD_IN_TILE
    tok = lambda n: pl.BlockSpec((tm, n), lambda t: (t, 0))
    return pl.pallas_call(
        functools.partial(_ssd_in_kernel, seg=seg),
        grid=(t_tokens // tm,),
        in_specs=[
            tok(D_MODEL),
            _mod_spec(layer, tokens_per_cond // tm, first_row),
            _layer_spec(layer, (1, D_MODEL)),
            _layer_spec(j, (D_IN_PROJ, D_MODEL)),
            _layer_spec(j, (3, D_XBC)),
            _layer_spec(j, (1, D_XBC)),
            _layer_spec(j, (2 * N_HEADS, LANES)),
        ],
        out_specs=[
            tok(D_INNER), tok(D_INNER), tok(D_BC), tok(D_BC),
            pl.BlockSpec((tm // CHUNK, 2 * N_HEADS, LANES), lambda t: (t, 0, 0)),
        ],
        out_shape=[
            jax.ShapeDtypeStruct((t_tokens, D_INNER), BF16),
            jax.ShapeDtypeStruct((t_tokens, D_INNER), BF16),
            jax.ShapeDtypeStruct((t_tokens, D_BC), BF16),
            jax.ShapeDtypeStruct((t_tokens, D_BC), BF16),
            jax.ShapeDtypeStruct((t_tokens // CHUNK, 2 * N_HEADS, LANES), F32),
        ],
        scratch_shapes=[pltpu.VMEM((D_ZX, D_MODEL), BF16), pltpu.VMEM((2 * N_HEADS, D_MODEL), BF16)],
        compiler_params=_params("arbitrary"),
        name="ssd_in",
    )(x, mods, norm_w, w_in, conv_w, conv_b, dtb)


def _split3(a):
    hi = a.astype(BF16)
    r = a - hi.astype(F32)
    mid = r.astype(BF16)
    lo = (r - mid.astype(F32)).astype(BF16)
    return hi, mid, lo


def _dot3(a, m):
    hi, mid, lo = _split3(a)
    return _dot(hi, m) + _dot(mid, m) + _dot(lo, m)


def _ssd_scan_kernel(*refs, seq_len, seqs, has_h0, emit_state, has_alias):
    xs_ref, b_ref, c_ref, dt_ref, alog_ref, dskip_ref = refs[:6]
    pos = 6
    h0_ref = refs[pos] if has_h0 else None
    pos += int(has_h0) + int(has_alias)
    y_ref = refs[pos]
    hf_ref = refs[pos + 1] if emit_state else None
    ht_ref = refs[-1]

    nc = seq_len // CHUNK
    hg = HEADS_PER_GROUP
    rows = lax.broadcasted_iota(jnp.int32, (CHUNK, CHUNK), 0)
    lanes = lax.broadcasted_iota(jnp.int32, (CHUNK, CHUNK), 1)
    tril = rows >= lanes
    triu = rows <= lanes
    tril_b = jnp.where(tril, 1.0, 0.0).astype(BF16)
    triu_b = jnp.where(triu, 1.0, 0.0).astype(BF16)
    ones_b = jnp.ones((CHUNK, CHUNK), BF16)
    low_half = lanes < HEAD_DIM
    low_b = jnp.where(low_half, 1.0, 0.0).astype(BF16)
    high_b = jnp.where(low_half, 0.0, 1.0).astype(BF16)
    a2 = -jnp.exp(alog_ref[...]) * LOG2E

    sum_tot_b = [jnp.concatenate([triu_b, ones_b], axis=1), jnp.concatenate([tril_b, ones_b], axis=1)]

    def scan_short(s):
        below = rows > lanes
        above = rows < lanes
        y, st, cols, cdp, c16 = {}, {}, {}, {}, {}
        for c in range(nc):
            t0 = s * seq_len + c * CHUNK
            dt16 = dt_ref[s * nc + c]
            da16 = dt16 * a2
            parts = jnp.concatenate(_split3(da16), axis=0)
            dts, das, cs_t, w_t, cd, src_t = [], [], [], [], [], []
            for d in range(2):
                res = _dot(parts, sum_tot_b[d])
                sums = (res[0:2 * hg] + res[2 * hg:4 * hg] + res[4 * hg:6 * hg])[d * hg:(d + 1) * hg]
                dts.append(dt16[d * hg:(d + 1) * hg])
                das.append(da16[d * hg:(d + 1) * hg])
                cs_t.append(sums[:, 0:CHUNK])
                tot = sums[:, CHUNK:2 * CHUNK]
                w_t.append(dts[d] * jnp.exp2(tot - cs_t[d]))
                cd.append(jnp.exp2(tot))
                src_t.append(cs_t[d] - jnp.log(dts[d]) * LOG2E)
            diag_t = jnp.log(dts[0] + dts[1]) * LOG2E
            bc = b_ref[t0:t0 + CHUNK, :]
            c16[c] = c_ref[t0:t0 + CHUNK, :]
            cb = _dot_nt(c16[c], bc)
            b_t = bc.astype(F32).T.astype(BF16)
            w16 = [w.astype(BF16) for w in w_t]
            for q in range(hg // 2):
                sl = slice(q * LANES, (q + 1) * LANES)
                xpb = xs_ref[t0:t0 + CHUNK, sl]
                x2 = jnp.concatenate([xpb * low_b, xpb * high_b], axis=0)
                m_parts, bw_parts, col_parts = [], ([], []), ([], [])
                for r in (2 * q, 2 * q + 1):
                    col_f = jnp.sum(jnp.where(tril, das[0][r:r + 1, :], 0.0), axis=1, keepdims=True)
                    col_b = jnp.sum(jnp.where(triu, das[1][r:r + 1, :], 0.0), axis=1, keepdims=True)
                    arg = jnp.where(below, col_f - src_t[0][r:r + 1, :],
                                    jnp.where(above, col_b - src_t[1][r:r + 1, :], diag_t[r:r + 1, :]))
                    m_parts.append((cb * jnp.exp2(arg)).astype(BF16))
                    for d, col in ((0, col_f), (1, col_b)):
                        bw_parts[d].append(b_t * w16[d][r:r + 1, :])
                        col_parts[d].append(col)
                y[c, q] = _dot(jnp.concatenate(m_parts, axis=1), x2) + xpb.astype(F32) * dskip_ref[:, sl]
                for d in range(2):
                    st[d, c, q] = _dot(jnp.concatenate(bw_parts[d], axis=1), x2)
                    cols[d, c, q] = col_parts[d]
                    cdp[d, c, q] = jnp.where(low_half, cd[d][2 * q:2 * q + 1, :], cd[d][2 * q + 1:2 * q + 2, :])

        for d in range(2):
            order = range(nc) if d == 0 else range(nc - 1, -1, -1)
            for q in range(hg // 2):
                sl = slice(q * LANES, (q + 1) * LANES)
                h = h0_ref[s, d, sl, :].T if has_h0 else None
                for c in order:
                    if h is None:
                        h = st[d, c, q]
                    else:
                        ecs = jnp.exp2(jnp.where(low_half, cols[d, c, q][0], cols[d, c, q][1]))
                        y[c, q] = y[c, q] + _dot(c16[c], h.astype(BF16)) * ecs
                        h = h * cdp[d, c, q] + st[d, c, q]
                if emit_state:
                    hf_ref[s, d, sl, :] = h.T
        for c in range(nc):
            for q in range(hg // 2):
                t0 = s * seq_len + c * CHUNK
                y_ref[t0:t0 + CHUNK, q * LANES:(q + 1) * LANES] = y[c, q].astype(BF16)

    if nc <= SHORT_SCAN_CHUNKS:
        for s in range(seqs):
            scan_short(s)
        return

    y_ref[...] = jnp.zeros(y_ref.shape, y_ref.dtype)
    for s in range(seqs):
        for d in range(2):
            for q in range(hg // 2):
                sl = slice(q * LANES, (q + 1) * LANES)
                if has_h0:
                    ht_ref[s, d, :, sl] = h0_ref[s, d, sl, :].T
                else:
                    ht_ref[s, d, :, sl] = jnp.zeros((D_STATE, LANES), F32)

    def chunk_step(s, d, c):
        mask = tril if d == 0 else triu
        t0 = s * seq_len + pl.multiple_of(c * CHUNK, CHUNK)
        dt16 = dt_ref[s * nc + c]
        da16 = dt16 * a2
        res = _dot(jnp.concatenate(_split3(da16), axis=0), sum_tot_b[d])
        sums = res[0:2 * hg] + res[2 * hg:4 * hg] + res[4 * hg:6 * hg]
        cs_t = sums[d * hg:(d + 1) * hg, 0:CHUNK]
        tot = sums[d * hg:(d + 1) * hg, CHUNK:2 * CHUNK]
        dt8 = dt16[d * hg:(d + 1) * hg]
        da8 = da16[d * hg:(d + 1) * hg]
        w_t = dt8 * jnp.exp2(tot - cs_t)
        cd = jnp.exp2(tot)
        src_t = cs_t - jnp.log(dt8) * LOG2E
        bc = b_ref[pl.ds(t0, CHUNK), :]
        cb16 = c_ref[pl.ds(t0, CHUNK), :]
        cb = _dot_nt(cb16, bc)
        b_t = bc.astype(F32).T

        for q in range(hg // 2):
            sl = slice(q * LANES, (q + 1) * LANES)
            xpb = xs_ref[pl.ds(t0, CHUNK), sl]
            x2 = jnp.concatenate([xpb * low_b, xpb * high_b], axis=0)
            m_parts, bw_parts, c_parts = [], [], []
            for r in (2 * q, 2 * q + 1):
                cs_col = jnp.sum(jnp.where(mask, da8[r:r + 1, :], 0.0), axis=1, keepdims=True)
                decay = jnp.where(mask, jnp.exp2(cs_col - src_t[r:r + 1, :]), 0.0)
                m_parts.append((cb * decay).astype(BF16))
                bw_parts.append((b_t * w_t[r:r + 1, :]).astype(BF16))
                c_parts.append(cs_col)
            y = _dot(jnp.concatenate(m_parts, axis=1), x2) + y_ref[pl.ds(t0, CHUNK), sl].astype(F32)
            st = _dot(jnp.concatenate(bw_parts, axis=1), x2)
            htp = ht_ref[s, d, :, sl]
            ecs = jnp.exp2(jnp.where(low_half, c_parts[0], c_parts[1]))
            cdp = jnp.where(low_half, cd[2 * q:2 * q + 1, :], cd[2 * q + 1:2 * q + 2, :])
            y = y + _dot(cb16, htp.astype(BF16)) * ecs
            ht_ref[s, d, :, sl] = htp * cdp + st
            if d == 0:
                y = y + xpb.astype(F32) * dskip_ref[:, sl]
            y_ref[pl.ds(t0, CHUNK), sl] = y.astype(BF16)

    def body(i, carry):
        for s in range(seqs):
            chunk_step(s, 0, i)
            chunk_step(s, 1, nc - 1 - i)
        return carry

    lax.fori_loop(0, nc, body, 0, unroll=2)

    if emit_state:
        for s in range(seqs):
            for d in range(2):
                for q in range(hg // 2):
                    sl = slice(q * LANES, (q + 1) * LANES)
                    hf_ref[s, d, sl, :] = ht_ref[s, d, :, sl].T


def _ssd_scan(xs, bm, cm, dt, alog, dskip, seq_len, seqs, ssd_index, h0=None, state_out=None):
    t_tokens = xs.shape[0]
    rows = seq_len * seqs
    nb = t_tokens // rows
    gw = HEADS_PER_GROUP * HEAD_DIM
    nc = seq_len // CHUNK
    in_specs = [
        pl.BlockSpec((rows, gw), lambda b, g: (b, g)),
        pl.BlockSpec((rows, D_STATE), lambda b, g: (b, g)),
        pl.BlockSpec((rows, D_STATE), lambda b, g: (b, g)),
        pl.BlockSpec((seqs * nc, 2 * HEADS_PER_GROUP, LANES), lambda b, g: (b, g, 0)),
        pl.BlockSpec((None, 2 * HEADS_PER_GROUP, LANES), lambda b, g: (ssd_index, g, 0)),
        pl.BlockSpec((None, 1, gw), lambda b, g: (ssd_index, 0, g)),
    ]
    args = [xs, bm, cm, dt, alog, dskip]
    has_h0 = h0 is not None
    if has_h0:
        in_specs.append(pl.BlockSpec((seqs, None, 2, None, gw, D_STATE),
                                     lambda b, g: (b, ssd_index, 0, g, 0, 0)))
        args.append(h0)
    emit_state = state_out is not None
    has_alias = emit_state and not isinstance(state_out, jax.ShapeDtypeStruct)
    out_specs = [pl.BlockSpec((rows, gw), lambda b, g: (b, g))]
    out_shape = [jax.ShapeDtypeStruct((t_tokens, D_INNER), BF16)]
    aliases = {}
    if emit_state:
        if has_alias:
            in_specs.append(pl.BlockSpec(memory_space=pl.ANY))
            args.append(state_out)
            aliases = {len(args) - 1: 1}
        state_sds = jax.ShapeDtypeStruct(state_out.shape, state_out.dtype)
        out_specs.append(pl.BlockSpec((seqs, None, 2, None, gw, D_STATE),
                                      lambda b, g: (b, ssd_index, 0, g, 0, 0)))
        out_shape.append(state_sds)
    res = pl.pallas_call(
        functools.partial(_ssd_scan_kernel, seq_len=seq_len, seqs=seqs, has_h0=has_h0,
                          emit_state=emit_state, has_alias=has_alias),
        grid=(nb, N_GROUPS),
        in_specs=in_specs,
        out_specs=out_specs,
        out_shape=out_shape,
        scratch_shapes=[pltpu.VMEM((seqs, 2, D_STATE, gw), F32)],
        input_output_aliases=aliases,
        compiler_params=_params("parallel", "parallel"),
        name="ssd_scan",
    )(*args)
    return res if emit_state else (res[0], None)


def _mlp_input(x, mod_ref, nw_ref):
    h = _rms(x) * nw_ref[...]
    return (h * (1.0 + mod_ref[4:5, :]) + mod_ref[3:4, :]).astype(BF16)


def _ssd_out_kernel(y_ref, z_ref, x_ref, mod_ref, nw_ref, wo_ref, nwm_ref, o_ref, hb_ref, wbf_ref):
    @pl.when(pl.program_id(0) == 0)
    def _():
        nw_cols = jnp.concatenate([nw_ref[...]] * (XBC_TILE // LANES), axis=1)
        for j in range(0, D_MODEL, XBC_TILE):
            wbf_ref[:, j:j + XBC_TILE] = (wo_ref[:, j:j + XBC_TILE] * nw_cols).astype(BF16)

    for r0 in range(0, x_ref.shape[0], SSD_OUT_ROWS):
        rs = slice(r0, r0 + SSD_OUT_ROWS)
        g = y_ref[rs, :] * _silu(z_ref[rs, :])
        gf = g.astype(F32)
        scale = lax.rsqrt(jnp.mean(gf * gf, axis=-1, keepdims=True) + EPS)
        x1 = x_ref[rs, :] + mod_ref[2:3, :] * (_dot(g, wbf_ref[...]) * scale)
        o_ref[rs, :] = x1
        hb_ref[rs, :] = _mlp_input(x1, mod_ref, nwm_ref)


def _ssd_out(y, z, x, mods, layer, j, first_row, tokens_per_cond, norm_w, w_out, norm_mlp):
    t_tokens = x.shape[0]
    tm = SSD_OUT_TILE
    tok = lambda n: pl.BlockSpec((tm, n), lambda t: (t, 0))
    return pl.pallas_call(
        _ssd_out_kernel,
        grid=(t_tokens // tm,),
        in_specs=[tok(D_INNER), tok(D_INNER), tok(D_MODEL),
                  _mod_spec(layer, tokens_per_cond // tm, first_row),
                  _layer_spec(j, (D_INNER, LANES)), _layer_spec(j, (D_INNER, D_MODEL)),
                  _layer_spec(layer, (1, D_MODEL))],
        out_specs=[tok(D_MODEL), tok(D_MODEL)],
        out_shape=[jax.ShapeDtypeStruct((t_tokens, D_MODEL), F32),
                   jax.ShapeDtypeStruct((t_tokens, D_MODEL), BF16)],
        scratch_shapes=[pltpu.VMEM((D_INNER, D_MODEL), BF16)],
        compiler_params=_params("arbitrary"),
        name="ssd_out",
    )(y, z, x, mods, norm_w, w_out, norm_mlp)


def _mlp_kernel(x_ref, hb_ref, mod_ref, w1_ref, w2_ref, fw_ref, o_ref, acc_ref, *, final):
    k = pl.program_id(1)
    last = pl.num_programs(1) - 1

    def block():
        a = jnp.maximum(_dot(hb_ref[...], w1_ref[...].astype(BF16)), 0.0)
        return _dot((a * a).astype(BF16), w2_ref[...].astype(BF16))

    @pl.when(k == 0)
    def _():
        acc_ref[...] = block()

    @pl.when(jnp.logical_and(k > 0, k < last))
    def _():
        acc_ref[...] += block()

    @pl.when(k == last)
    def _():
        x2 = x_ref[...] + mod_ref[5:6, :] * (acc_ref[...] + block())
        if final:
            x2 = _rms(x2) * fw_ref[...]
        o_ref[...] = x2


def _mlp(x, hb, mods, layer, first_row, tokens_per_cond, w1, w2, fw, final):
    t_tokens = x.shape[0]
    tm = MLP_TILE
    assert D_FF // FF_TILE >= 2
    tok = pl.BlockSpec((tm, D_MODEL), lambda t, k: (t, 0))
    return pl.pallas_call(
        functools.partial(_mlp_kernel, final=final),
        grid=(t_tokens // tm, D_FF // FF_TILE),
        in_specs=[tok, tok, _mod_spec(layer, tokens_per_cond // tm, first_row),
                  pl.BlockSpec((None, D_MODEL, FF_TILE), lambda t, k: (layer, 0, k)),
                  pl.BlockSpec((None, FF_TILE, D_MODEL), lambda t, k: (layer, k, 0)),
                  _const_spec((1, D_MODEL))],
        out_specs=tok,
        out_shape=jax.ShapeDtypeStruct((t_tokens, D_MODEL), F32),
        scratch_shapes=[pltpu.VMEM((tm, D_MODEL), F32)],
        compiler_params=_params("parallel", "arbitrary"),
        name="mlp",
    )(x, hb, mods, w1, w2, fw)


@functools.lru_cache(maxsize=None)
def _dft_matrices(n):
    k = np.arange(n)
    ang = 2.0 * np.pi * ((k[:, None] * k[None, :]) % n) / n
    return np.cos(ang) / np.sqrt(n), np.sin(ang) / np.sqrt(n)


def _fnet_kernel(x_ref, mod_ref, nw_ref, chan_ref, seqm_ref, wf_ref, bf_ref, nwm_ref, o_ref, hb_ref,
                 f_ref, wbf_ref, *, seq_len):
    @pl.when(pl.program_id(0) == 0)
    def _():
        _cast_weight(wbf_ref, wf_ref, D_MODEL)

    x = x_ref[...]
    h = _rms(x) * nw_ref[...]
    hb = (h * (1.0 + mod_ref[1:2, :]) + mod_ref[0:1, :]).astype(BF16)
    for g in range(N_FGROUPS):
        cols = slice(g * D_FG, (g + 1) * D_FG)
        p = _dot(hb[:, cols], chan_ref[...]).astype(BF16)
        for r0 in range(0, x_ref.shape[0], seq_len):
            rs = slice(r0, r0 + seq_len)
            stacked = jnp.concatenate([p[rs, :D_FG], p[rs, D_FG:]], axis=0)
            f_ref[rs, cols] = _dot(seqm_ref[...], stacked).astype(BF16)
    x1 = x + mod_ref[2:3, :] * (_dot(f_ref[...], wbf_ref[...]) + bf_ref[...])
    o_ref[...] = x1
    hb_ref[...] = _mlp_input(x1, mod_ref, nwm_ref)


def _fnet(x, mods, layer, j, first_row, seqs_per_cond, seq_len, seqs, norm_w, w_f, b_f, norm_mlp):
    t_tokens = x.shape[0]
    rows = seq_len * seqs
    cc, sc = _dft_matrices(D_FG)
    cl, sl = _dft_matrices(seq_len)
    chan = jnp.asarray(np.concatenate([cc, sc], axis=1), F32).astype(BF16)
    seqm = jnp.asarray(np.concatenate([cl, -sl], axis=1), F32).astype(BF16)
    tok = pl.BlockSpec((rows, D_MODEL), lambda t: (t, 0))
    return pl.pallas_call(
        functools.partial(_fnet_kernel, seq_len=seq_len),
        grid=(t_tokens // rows,),
        in_specs=[tok, _mod_spec(layer, seqs_per_cond // seqs, first_row), _layer_spec(layer, (1, D_MODEL)),
                  _const_spec((D_FG, 2 * D_FG)), _const_spec((seq_len, 2 * seq_len)),
                  _layer_spec(j, (D_MODEL, D_MODEL)), _layer_spec(j, (1, D_MODEL)),
                  _layer_spec(layer, (1, D_MODEL))],
        out_specs=[tok, tok],
        out_shape=[jax.ShapeDtypeStruct((t_tokens, D_MODEL), F32),
                   jax.ShapeDtypeStruct((t_tokens, D_MODEL), BF16)],
        scratch_shapes=[pltpu.VMEM((rows, D_MODEL), BF16), pltpu.VMEM((D_MODEL, D_MODEL), BF16)],
        compiler_params=_params("arbitrary"),
        name="fnet",
    )(x, mods, norm_w, chan, seqm, w_f, b_f, norm_mlp)


def _head_perm():
    return np.array([d * N_HEADS + g * HEADS_PER_GROUP + r
                     for g in range(N_GROUPS) for d in range(2) for r in range(HEADS_PER_GROUP)])


def kernel(x_prompt, x_sample, state_ssd, c, c_ctx, ada_w, ada_b, norm_mix_w, norm_mlp_w, ssd_w_in, ssd_conv_w, ssd_conv_b, ssd_dt_bias, ssd_a_log, ssd_d, ssd_norm_w, ssd_w_out, fno_w_out, fno_b_out, mlp_w1, mlp_w2, final_norm_w):
    batch, seq, _ = x_prompt.shape
    dec_batch, dec_seq, _ = x_sample.shape
    n_ssd = ssd_w_in.shape[0]

    cond = jnp.concatenate([c_ctx[None], c, jnp.zeros((N_COND - 1 - dec_batch, D_MODEL), F32)], axis=0)
    mods = _ada_modulation(cond, ada_w, ada_b)

    perm = _head_perm()
    lane_rep = lambda v: jnp.broadcast_to(v.reshape(n_ssd, 2 * N_HEADS)[:, perm][:, :, None],
                                          (n_ssd, 2 * N_HEADS, LANES))
    dtb = lane_rep(ssd_dt_bias)
    alog = lane_rep(ssd_a_log)
    dskip = jnp.repeat(ssd_d, HEAD_DIM, axis=1)[:, None, :]
    rows3 = lambda v: v[:, None, :]
    w_in_t = jnp.swapaxes(ssd_w_in, 1, 2)
    norm_mix, norm_mlp = rows3(norm_mix_w), rows3(norm_mlp_w)
    conv_b, fno_b = rows3(ssd_conv_b), rows3(fno_b_out)
    ssd_nw = jnp.broadcast_to(ssd_norm_w[:, :, None], (n_ssd, D_INNER, LANES))
    final_w = final_norm_w[None, :]

    gw = HEADS_PER_GROUP * HEAD_DIM
    new_state = jax.ShapeDtypeStruct((batch, n_ssd, 2, N_GROUPS, gw, D_STATE), F32)
    h0_all = state_ssd.reshape(dec_batch, n_ssd, 2, N_GROUPS, gw, D_STATE)

    streams = [
        dict(first_row=0, tokens_per_cond=batch * seq, seqs_per_cond=batch, seg=seq, seq_len=seq,
             scan_seqs=SCAN_SEQS if batch % SCAN_SEQS == 0 else 1,
             fnet_seqs=FNET_SEQS if batch % FNET_SEQS == 0 else 1),
        dict(first_row=1, tokens_per_cond=dec_seq, seqs_per_cond=1, seg=GRID_W, seq_len=dec_seq, scan_seqs=1,
             fnet_seqs=1),
    ]
    acts = [x_prompt.reshape(batch * seq, D_MODEL), x_sample.reshape(dec_batch * dec_seq, D_MODEL)]
    for i in range(DEPTH):
        j = i // 2
        for s, st in enumerate(streams):
            x = acts[s]
            fr, tpc = st["first_row"], st["tokens_per_cond"]
            if i % 2 == 0:
                z, xc, bm, cm, dt = _ssd_in(x, mods, i, j, fr, tpc, st["seg"], norm_mix, w_in_t,
                                            ssd_conv_w, conv_b, dtb)
                if s == 0:
                    y, new_state = _ssd_scan(xc, bm, cm, dt, alog, dskip, st["seq_len"], st["scan_seqs"], j,
                                             state_out=new_state)
                else:
                    y, _ = _ssd_scan(xc, bm, cm, dt, alog, dskip, st["seq_len"], st["scan_seqs"], j, h0=h0_all)
                x, hb = _ssd_out(y, z, x, mods, i, j, fr, tpc, ssd_nw, ssd_w_out, norm_mlp)
            else:
                x, hb = _fnet(x, mods, i, j, fr, st["seqs_per_cond"], st["seq_len"], st["fnet_seqs"], norm_mix,
                              fno_w_out, fno_b, norm_mlp)
            acts[s] = _mlp(x, hb, mods, i, fr, tpc, mlp_w1, mlp_w2, final_w, i == DEPTH - 1)

    y_prompt = acts[0].reshape(batch, seq, D_MODEL)
    y_sample = acts[1].reshape(dec_batch, dec_seq, D_MODEL)
    new_state_ssd = new_state.reshape(batch, n_ssd, 2, N_HEADS, HEAD_DIM, D_STATE)
    return (y_prompt, y_sample, new_state_ssd)
```

```python
import functools

import numpy as np
import jax
import jax.numpy as jnp
from jax import lax
from jax.experimental import pallas as pl
from jax.experimental.pallas import tpu as pltpu

D_MODEL = 1024
DEPTH = 4
GRID_W = 64
D_INNER = 2048
HEAD_DIM = 64
N_HEADS = 32
N_GROUPS = 4
HEADS_PER_GROUP = N_HEADS // N_GROUPS
HG_SHIFT = HEADS_PER_GROUP.bit_length() - 1
D_STATE = 128
CHUNK = 128
D_BC = N_GROUPS * D_STATE
D_XBC = D_INNER + 2 * D_BC
D_ZX = D_INNER + D_XBC
D_IN_PROJ = D_ZX + 2 * N_HEADS
N_FGROUPS = 4
D_FG = D_MODEL // N_FGROUPS
D_FF = 4 * D_MODEL
EPS = 1e-6
N_COND = 8
LOG2E = 1.4426950408889634

LANES = 128
SSD_IN_TILE = 256
SSD_OUT_TILE = 512
SSD_OUT_ROWS = 512
MLP_TILE = 1024
SCAN_SEQS = 4
FNET_SEQS = 4
MAX_SCAN_CHUNKS = 8
XBC_TILE = 512
FF_TILE = 1024
ADA_TILE = 1536
VMEM_LIMIT = 56 * 1024 * 1024

F32 = jnp.float32
BF16 = jnp.bfloat16


def _dot(a, b):
    return jnp.dot(a, b, preferred_element_type=F32)


def _dot_nt(a, b):
    return lax.dot_general(a, b, (((1,), (1,)), ((), ())), preferred_element_type=F32)


def _silu(x):
    h = 0.5 * x
    return h + h * jnp.tanh(h)


def _rms(x):
    return x * lax.rsqrt(jnp.mean(x * x, axis=-1, keepdims=True) + EPS)


def _const_spec(shape):
    nd = len(shape)
    return pl.BlockSpec(shape, lambda *_: (0,) * nd, pipeline_mode=pl.Buffered(1))


def _layer_spec(layer, tail):
    nd = len(tail)
    return pl.BlockSpec((None,) + tuple(tail), lambda *_: (layer,) + (0,) * nd,
                        pipeline_mode=pl.Buffered(1))


def _mod_spec(layer, tiles_per_cond, first_row):
    return pl.BlockSpec((None, None, 6, D_MODEL),
                        lambda t, *_: (layer, first_row + t // tiles_per_cond, 0, 0))


def _params(*semantics):
    return pltpu.CompilerParams(dimension_semantics=semantics, vmem_limit_bytes=VMEM_LIMIT)


def _cast_weight(dst_ref, src_ref, n_cols, tile=512):
    for j in range(0, n_cols, tile):
        dst_ref[:, j:j + tile] = src_ref[:, j:j + tile].astype(BF16)


def _ada_kernel(c_ref, w_ref, b_ref, o_ref):
    c = c_ref[...]
    s = _silu(c).astype(BF16)
    o_ref[...] = _dot(s, w_ref[...].astype(BF16)) + b_ref[...]


def _ada_modulation(cond, ada_w, ada_b):
    n_out = 6 * D_MODEL
    out = pl.pallas_call(
        _ada_kernel,
        grid=(DEPTH, n_out // ADA_TILE),
        in_specs=[
            pl.BlockSpec((N_COND, D_MODEL), lambda i, n: (0, 0)),
            pl.BlockSpec((None, D_MODEL, ADA_TILE), lambda i, n: (i, 0, n)),
            pl.BlockSpec((None, 1, ADA_TILE), lambda i, n: (i, 0, n)),
        ],
        out_specs=pl.BlockSpec((None, N_COND, ADA_TILE), lambda i, n: (i, 0, n)),
        out_shape=jax.ShapeDtypeStruct((DEPTH, N_COND, n_out), F32),
        compiler_params=_params("parallel", "parallel"),
        name="ada_mod",
    )(cond, ada_w, ada_b.reshape(DEPTH, 1, n_out))
    return out.reshape(DEPTH, N_COND, 6, D_MODEL)


def _ssd_in_kernel(x_ref, mod_ref, nw_ref, win_ref, cw_ref, cb_ref, dtb_ref,
                   z_ref, xs_ref, b_ref, c_ref, dt_ref, wbf_ref, wdt_ref, *, seg):
    @pl.when(pl.program_id(0) == 0)
    def _():
        for r0 in range(0, D_ZX, XBC_TILE):
            wbf_ref[r0:r0 + XBC_TILE, :] = win_ref[r0:r0 + XBC_TILE, :].astype(BF16)
        n = lax.broadcasted_iota(jnp.int32, (2 * N_HEADS, 2 * N_HEADS), 0)
        k = lax.broadcasted_iota(jnp.int32, (2 * N_HEADS, 2 * N_HEADS), 1)
        src = (((n >> HG_SHIFT) & 1) * N_HEADS + (n >> (HG_SHIFT + 1)) * HEADS_PER_GROUP
               + (n & (HEADS_PER_GROUP - 1)))
        pick = jnp.where(k == src, 1.0, 0.0).astype(BF16)
        wdt_ref[...] = _dot(pick, win_ref[D_ZX:D_IN_PROJ, :].astype(BF16)).astype(BF16)

    tm = x_ref.shape[0]
    h = _rms(x_ref[...]) * nw_ref[...]
    h = h * (1.0 + mod_ref[1:2, :]) + mod_ref[0:1, :]
    hb = h.astype(BF16)
    for j in range(D_INNER // XBC_TILE):
        cols = slice(j * XBC_TILE, (j + 1) * XBC_TILE)
        z_ref[:, cols] = _dot_nt(hb, wbf_ref[cols, :]).astype(BF16)

    row = lax.broadcasted_iota(jnp.int32, (tm, XBC_TILE), 0) & (seg - 1)
    first = row == 0
    last = row == seg - 1
    for j in range(D_XBC // XBC_TILE):
        cols = slice(j * XBC_TILE, (j + 1) * XBC_TILE)
        acc = _dot_nt(hb, wbf_ref[D_INNER + j * XBC_TILE:D_INNER + (j + 1) * XBC_TILE, :])
        up = jnp.where(first, 0.0, pltpu.roll(acc, 1, axis=0))
        dn = jnp.where(last, 0.0, pltpu.roll(acc, tm - 1, axis=0))
        o = up * cw_ref[0:1, cols] + acc * cw_ref[1:2, cols] + dn * cw_ref[2:3, cols] + cb_ref[:, cols]
        o = _silu(o.astype(BF16))
        if j < D_INNER // XBC_TILE:
            xs_ref[:, cols] = o
        elif j == D_INNER // XBC_TILE:
            b_ref[...] = o
        else:
            c_ref[...] = o

    dtT = _dot_nt(wdt_ref[...], hb)
    for k in range(tm // CHUNK):
        v = dtT[:, k * CHUNK:(k + 1) * CHUNK] + dtb_ref[...]
        dt_ref[k] = jnp.maximum(v, 0.0) + jnp.log1p(jnp.exp(-jnp.abs(v)))


def _ssd_in(x, mods, layer, j, first_row, tokens_per_cond, seg, norm_w, w_in, conv_w, conv_b, dtb):
    t_tokens = x.shape[0]
    tm = SSD_IN_TILE
    tok = lambda n: pl.BlockSpec((tm, n), lambda t: (t, 0))
    return pl.pallas_call(
        functools.partial(_ssd_in_kernel, seg=seg),
        grid=(t_tokens // tm,),
        in_specs=[
            tok(D_MODEL),
            _mod_spec(layer, tokens_per_cond // tm, first_row),
            _layer_spec(layer, (1, D_MODEL)),
            _layer_spec(j, (D_IN_PROJ, D_MODEL)),
            _layer_spec(j, (3, D_XBC)),
            _layer_spec(j, (1, D_XBC)),
            _layer_spec(j, (2 * N_HEADS, LANES)),
        ],
        out_specs=[
            tok(D_INNER), tok(D_INNER), tok(D_BC), tok(D_BC),
            pl.BlockSpec((tm // CHUNK, 2 * N_HEADS, LANES), lambda t: (t, 0, 0)),
        ],
        out_shape=[
            jax.ShapeDtypeStruct((t_tokens, D_INNER), BF16),
            jax.ShapeDtypeStruct((t_tokens, D_INNER), BF16),
            jax.ShapeDtypeStruct((t_tokens, D_BC), BF16),
            jax.ShapeDtypeStruct((t_tokens, D_BC), BF16),
            jax.ShapeDtypeStruct((t_tokens // CHUNK, 2 * N_HEADS, LANES), F32),
        ],
        scratch_shapes=[pltpu.VMEM((D_ZX, D_MODEL), BF16), pltpu.VMEM((2 * N_HEADS, D_MODEL), BF16)],
        compiler_params=_params("arbitrary"),
        name="ssd_in",
    )(x, mods, norm_w, w_in, conv_w, conv_b, dtb)


def _split3(a):
    hi = a.astype(BF16)
    r = a - hi.astype(F32)
    mid = r.astype(BF16)
    lo = (r - mid.astype(F32)).astype(BF16)
    return hi, mid, lo


def _dot3(a, m):
    hi, mid, lo = _split3(a)
    return _dot(hi, m) + _dot(mid, m) + _dot(lo, m)


def _ssd_scan_kernel(*refs, seq_len, seqs, has_h0, emit_state, has_alias):
    xs_ref, b_ref, c_ref, dt_ref, alog_ref, dskip_ref = refs[:6]
    pos = 6
    h0_ref = refs[pos] if has_h0 else None
    pos += int(has_h0) + int(has_alias)
    y_ref = refs[pos]
    hf_ref = refs[pos + 1] if emit_state else None

    nc = seq_len // CHUNK
    hg = HEADS_PER_GROUP
    rows = lax.broadcasted_iota(jnp.int32, (CHUNK, CHUNK), 0)
    lanes = lax.broadcasted_iota(jnp.int32, (CHUNK, CHUNK), 1)
    tril = rows >= lanes
    triu = rows <= lanes
    tril_b = jnp.where(tril, 1.0, 0.0).astype(BF16)
    triu_b = jnp.where(triu, 1.0, 0.0).astype(BF16)
    ones_b = jnp.ones((CHUNK, CHUNK), BF16)
    low_half = lanes < HEAD_DIM
    low_b = jnp.where(low_half, 1.0, 0.0).astype(BF16)
    high_b = jnp.where(low_half, 0.0, 1.0).astype(BF16)
    a2 = -jnp.exp(alog_ref[...]) * LOG2E

    sum_tot_b = [jnp.concatenate([triu_b, ones_b], axis=1), jnp.concatenate([tril_b, ones_b], axis=1)]

    def scan_sequence(s):
        below = rows > lanes
        above = rows < lanes
        y, st, cols, cdp, c16 = {}, {}, {}, {}, {}
        for c in range(nc):
            t0 = s * seq_len + c * CHUNK
            dt16 = dt_ref[s * nc + c]
            da16 = dt16 * a2
            parts = jnp.concatenate(_split3(da16), axis=0)
            dts, das, cs_t, w_t, cd, src_t = [], [], [], [], [], []
            for d in range(2):
                res = _dot(parts, sum_tot_b[d])
                sums = (res[0:2 * hg] + res[2 * hg:4 * hg] + res[4 * hg:6 * hg])[d * hg:(d + 1) * hg]
                dts.append(dt16[d * hg:(d + 1) * hg])
                das.append(da16[d * hg:(d + 1) * hg])
                cs_t.append(sums[:, 0:CHUNK])
                tot = sums[:, CHUNK:2 * CHUNK]
                w_t.append(dts[d] * jnp.exp2(tot - cs_t[d]))
                cd.append(jnp.exp2(tot))
                src_t.append(cs_t[d] - jnp.log(dts[d]) * LOG2E)
            diag_t = jnp.log(dts[0] + dts[1]) * LOG2E
            bc = b_ref[t0:t0 + CHUNK, :]
            c16[c] = c_ref[t0:t0 + CHUNK, :]
            cb = _dot_nt(c16[c], bc)
            b_t = bc.astype(F32).T.astype(BF16)
            w16 = [w.astype(BF16) for w in w_t]
            for q in range(hg // 2):
                sl = slice(q * LANES, (q + 1) * LANES)
                xpb = xs_ref[t0:t0 + CHUNK, sl]
                x2 = jnp.concatenate([xpb * low_b, xpb * high_b], axis=0)
                m_parts, bw_parts, col_parts = [], ([], []), ([], [])
                for r in (2 * q, 2 * q + 1):
                    col_f = jnp.sum(jnp.where(tril, das[0][r:r + 1, :], 0.0), axis=1, keepdims=True)
                    col_b = jnp.sum(jnp.where(triu, das[1][r:r + 1, :], 0.0), axis=1, keepdims=True)
                    arg = jnp.where(below, col_f - src_t[0][r:r + 1, :],
                                    jnp.where(above, col_b - src_t[1][r:r + 1, :], diag_t[r:r + 1, :]))
                    m_parts.append((cb * jnp.exp2(arg)).astype(BF16))
                    for d, col in ((0, col_f), (1, col_b)):
                        bw_parts[d].append(b_t * w16[d][r:r + 1, :])
                        col_parts[d].append(col)
                y[c, q] = _dot(jnp.concatenate(m_parts, axis=1), x2) + xpb.astype(F32) * dskip_ref[:, sl]
                for d in range(2):
                    st[d, c, q] = _dot(jnp.concatenate(bw_parts[d], axis=1), x2)
                    cols[d, c, q] = col_parts[d]
                    cdp[d, c, q] = jnp.where(low_half, cd[d][2 * q:2 * q + 1, :], cd[d][2 * q + 1:2 * q + 2, :])

        for d in range(2):
            order = range(nc) if d == 0 else range(nc - 1, -1, -1)
            for q in range(hg // 2):
                sl = slice(q * LANES, (q + 1) * LANES)
                h = h0_ref[s, d, sl, :].T if has_h0 else None
                for c in order:
                    if h is None:
                        h = st[d, c, q]
                    else:
                        ecs = jnp.exp2(jnp.where(low_half, cols[d, c, q][0], cols[d, c, q][1]))
                        y[c, q] = y[c, q] + _dot(c16[c], h.astype(BF16)) * ecs
                        h = h * cdp[d, c, q] + st[d, c, q]
                if emit_state:
                    hf_ref[s, d, sl, :] = h.T
        for c in range(nc):
            for q in range(hg // 2):
                t0 = s * seq_len + c * CHUNK
                y_ref[t0:t0 + CHUNK, q * LANES:(q + 1) * LANES] = y[c, q].astype(BF16)

    for s in range(seqs):
        scan_sequence(s)


def _ssd_scan(xs, bm, cm, dt, alog, dskip, seq_len, seqs, ssd_index, h0=None, state_out=None):
    t_tokens = xs.shape[0]
    rows = seq_len * seqs
    nb = t_tokens // rows
    gw = HEADS_PER_GROUP * HEAD_DIM
    nc = seq_len // CHUNK
    assert seq_len % CHUNK == 0 and nc <= MAX_SCAN_CHUNKS and t_tokens % rows == 0
    in_specs = [
        pl.BlockSpec((rows, gw), lambda b, g: (b, g)),
        pl.BlockSpec((rows, D_STATE), lambda b, g: (b, g)),
        pl.BlockSpec((rows, D_STATE), lambda b, g: (b, g)),
        pl.BlockSpec((seqs * nc, 2 * HEADS_PER_GROUP, LANES), lambda b, g: (b, g, 0)),
        pl.BlockSpec((None, 2 * HEADS_PER_GROUP, LANES), lambda b, g: (ssd_index, g, 0)),
        pl.BlockSpec((None, 1, gw), lambda b, g: (ssd_index, 0, g)),
    ]
    args = [xs, bm, cm, dt, alog, dskip]
    has_h0 = h0 is not None
    if has_h0:
        in_specs.append(pl.BlockSpec((seqs, None, 2, None, gw, D_STATE),
                                     lambda b, g: (b, ssd_index, 0, g, 0, 0)))
        args.append(h0)
    emit_state = state_out is not None
    has_alias = emit_state and not isinstance(state_out, jax.ShapeDtypeStruct)
    out_specs = [pl.BlockSpec((rows, gw), lambda b, g: (b, g))]
    out_shape = [jax.ShapeDtypeStruct((t_tokens, D_INNER), BF16)]
    aliases = {}
    if emit_state:
        if has_alias:
            in_specs.append(pl.BlockSpec(memory_space=pl.ANY))
            args.append(state_out)
            aliases = {len(args) - 1: 1}
        state_sds = jax.ShapeDtypeStruct(state_out.shape, state_out.dtype)
        out_specs.append(pl.BlockSpec((seqs, None, 2, None, gw, D_STATE),
                                      lambda b, g: (b, ssd_index, 0, g, 0, 0)))
        out_shape.append(state_sds)
    res = pl.pallas_call(
        functools.partial(_ssd_scan_kernel, seq_len=seq_len, seqs=seqs, has_h0=has_h0,
                          emit_state=emit_state, has_alias=has_alias),
        grid=(nb, N_GROUPS),
        in_specs=in_specs,
        out_specs=out_specs,
        out_shape=out_shape,
        input_output_aliases=aliases,
        compiler_params=_params("parallel", "parallel"),
        name="ssd_scan",
    )(*args)
    return res if emit_state else (res[0], None)


def _mlp_input(x, mod_ref, nw_ref):
    h = _rms(x) * nw_ref[...]
    return (h * (1.0 + mod_ref[4:5, :]) + mod_ref[3:4, :]).astype(BF16)


def _ssd_out_kernel(y_ref, z_ref, x_ref, mod_ref, nw_ref, wo_ref, nwm_ref, o_ref, hb_ref, wbf_ref):
    @pl.when(pl.program_id(0) == 0)
    def _():
        nw_cols = jnp.concatenate([nw_ref[...]] * (XBC_TILE // LANES), axis=1)
        for j in range(0, D_MODEL, XBC_TILE):
            wbf_ref[:, j:j + XBC_TILE] = (wo_ref[:, j:j + XBC_TILE] * nw_cols).astype(BF16)

    for r0 in range(0, x_ref.shape[0], SSD_OUT_ROWS):
        rs = slice(r0, r0 + SSD_OUT_ROWS)
        g = y_ref[rs, :] * _silu(z_ref[rs, :])
        gf = g.astype(F32)
        scale = lax.rsqrt(jnp.mean(gf * gf, axis=-1, keepdims=True) + EPS)
        x1 = x_ref[rs, :] + mod_ref[2:3, :] * (_dot(g, wbf_ref[...]) * scale)
        o_ref[rs, :] = x1
        hb_ref[rs, :] = _mlp_input(x1, mod_ref, nwm_ref)


def _ssd_out(y, z, x, mods, layer, j, first_row, tokens_per_cond, norm_w, w_out, norm_mlp):
    t_tokens = x.shape[0]
    tm = SSD_OUT_TILE
    tok = lambda n: pl.BlockSpec((tm, n), lambda t: (t, 0))
    return pl.pallas_call(
        _ssd_out_kernel,
        grid=(t_tokens // tm,),
        in_specs=[tok(D_INNER), tok(D_INNER), tok(D_MODEL),
                  _mod_spec(layer, tokens_per_cond // tm, first_row),
                  _layer_spec(j, (D_INNER, LANES)), _layer_spec(j, (D_INNER, D_MODEL)),
                  _layer_spec(layer, (1, D_MODEL))],
        out_specs=[tok(D_MODEL), tok(D_MODEL)],
        out_shape=[jax.ShapeDtypeStruct((t_tokens, D_MODEL), F32),
                   jax.ShapeDtypeStruct((t_tokens, D_MODEL), BF16)],
        scratch_shapes=[pltpu.VMEM((D_INNER, D_MODEL), BF16)],
        compiler_params=_params("arbitrary"),
        name="ssd_out",
    )(y, z, x, mods, norm_w, w_out, norm_mlp)


def _mlp_kernel(x_ref, hb_ref, mod_ref, w1_ref, w2_ref, fw_ref, o_ref, acc_ref, *, final):
    k = pl.program_id(1)
    last = pl.num_programs(1) - 1

    def block():
        a = jnp.maximum(_dot(hb_ref[...], w1_ref[...].astype(BF16)), 0.0)
        return _dot((a * a).astype(BF16), w2_ref[...].astype(BF16))

    @pl.when(k == 0)
    def _():
        acc_ref[...] = block()

    @pl.when(jnp.logical_and(k > 0, k < last))
    def _():
        acc_ref[...] += block()

    @pl.when(k == last)
    def _():
        x2 = x_ref[...] + mod_ref[5:6, :] * (acc_ref[...] + block())
        if final:
            x2 = _rms(x2) * fw_ref[...]
        o_ref[...] = x2


def _mlp(x, hb, mods, layer, first_row, tokens_per_cond, w1, w2, fw, final):
    t_tokens = x.shape[0]
    tm = MLP_TILE
    assert D_FF // FF_TILE >= 2
    tok = pl.BlockSpec((tm, D_MODEL), lambda t, k: (t, 0))
    return pl.pallas_call(
        functools.partial(_mlp_kernel, final=final),
        grid=(t_tokens // tm, D_FF // FF_TILE),
        in_specs=[tok, tok, _mod_spec(layer, tokens_per_cond // tm, first_row),
                  pl.BlockSpec((None, D_MODEL, FF_TILE), lambda t, k: (layer, 0, k)),
                  pl.BlockSpec((None, FF_TILE, D_MODEL), lambda t, k: (layer, k, 0)),
                  _const_spec((1, D_MODEL))],
        out_specs=tok,
        out_shape=jax.ShapeDtypeStruct((t_tokens, D_MODEL), F32),
        scratch_shapes=[pltpu.VMEM((tm, D_MODEL), F32)],
        compiler_params=_params("parallel", "arbitrary"),
        name="mlp",
    )(x, hb, mods, w1, w2, fw)


@functools.lru_cache(maxsize=None)
def _dft_matrices(n):
    k = np.arange(n)
    ang = 2.0 * np.pi * ((k[:, None] * k[None, :]) % n) / n
    return np.cos(ang) / np.sqrt(n), np.sin(ang) / np.sqrt(n)


def _fnet_kernel(x_ref, mod_ref, nw_ref, chan_ref, seqm_ref, wf_ref, bf_ref, nwm_ref, o_ref, hb_ref,
                 f_ref, wbf_ref, *, seq_len):
    @pl.when(pl.program_id(0) == 0)
    def _():
        _cast_weight(wbf_ref, wf_ref, D_MODEL)

    x = x_ref[...]
    h = _rms(x) * nw_ref[...]
    hb = (h * (1.0 + mod_ref[1:2, :]) + mod_ref[0:1, :]).astype(BF16)
    for g in range(N_FGROUPS):
        cols = slice(g * D_FG, (g + 1) * D_FG)
        p = _dot(hb[:, cols], chan_ref[...]).astype(BF16)
        for r0 in range(0, x_ref.shape[0], seq_len):
            rs = slice(r0, r0 + seq_len)
            stacked = jnp.concatenate([p[rs, :D_FG], p[rs, D_FG:]], axis=0)
            f_ref[rs, cols] = _dot(seqm_ref[...], stacked).astype(BF16)
    x1 = x + mod_ref[2:3, :] * (_dot(f_ref[...], wbf_ref[...]) + bf_ref[...])
    o_ref[...] = x1
    hb_ref[...] = _mlp_input(x1, mod_ref, nwm_ref)


def _fnet(x, mods, layer, j, first_row, seqs_per_cond, seq_len, seqs, norm_w, w_f, b_f, norm_mlp):
    t_tokens = x.shape[0]
    rows = seq_len * seqs
    cc, sc = _dft_matrices(D_FG)
    cl, sl = _dft_matrices(seq_len)
    chan = jnp.asarray(np.concatenate([cc, sc], axis=1), F32).astype(BF16)
    seqm = jnp.asarray(np.concatenate([cl, -sl], axis=1), F32).astype(BF16)
    tok = pl.BlockSpec((rows, D_MODEL), lambda t: (t, 0))
    return pl.pallas_call(
        functools.partial(_fnet_kernel, seq_len=seq_len),
        grid=(t_tokens // rows,),
        in_specs=[tok, _mod_spec(layer, seqs_per_cond // seqs, first_row), _layer_spec(layer, (1, D_MODEL)),
                  _const_spec((D_FG, 2 * D_FG)), _const_spec((seq_len, 2 * seq_len)),
                  _layer_spec(j, (D_MODEL, D_MODEL)), _layer_spec(j, (1, D_MODEL)),
                  _layer_spec(layer, (1, D_MODEL))],
        out_specs=[tok, tok],
        out_shape=[jax.ShapeDtypeStruct((t_tokens, D_MODEL), F32),
                   jax.ShapeDtypeStruct((t_tokens, D_MODEL), BF16)],
        scratch_shapes=[pltpu.VMEM((rows, D_MODEL), BF16), pltpu.VMEM((D_MODEL, D_MODEL), BF16)],
        compiler_params=_params("arbitrary"),
        name="fnet",
    )(x, mods, norm_w, chan, seqm, w_f, b_f, norm_mlp)


def _head_perm():
    return np.array([d * N_HEADS + g * HEADS_PER_GROUP + r
                     for g in range(N_GROUPS) for d in range(2) for r in range(HEADS_PER_GROUP)])


def kernel(x_prompt, x_sample, state_ssd, c, c_ctx, ada_w, ada_b, norm_mix_w, norm_mlp_w, ssd_w_in, ssd_conv_w, ssd_conv_b, ssd_dt_bias, ssd_a_log, ssd_d, ssd_norm_w, ssd_w_out, fno_w_out, fno_b_out, mlp_w1, mlp_w2, final_norm_w):
    batch, seq, _ = x_prompt.shape
    dec_batch, dec_seq, _ = x_sample.shape
    n_ssd = ssd_w_in.shape[0]

    cond = jnp.concatenate([c_ctx[None], c, jnp.zeros((N_COND - 1 - dec_batch, D_MODEL), F32)], axis=0)
    mods = _ada_modulation(cond, ada_w, ada_b)

    perm = _head_perm()
    lane_rep = lambda v: jnp.broadcast_to(v.reshape(n_ssd, 2 * N_HEADS)[:, perm][:, :, None],
                                          (n_ssd, 2 * N_HEADS, LANES))
    dtb = lane_rep(ssd_dt_bias)
    alog = lane_rep(ssd_a_log)
    dskip = jnp.repeat(ssd_d, HEAD_DIM, axis=1)[:, None, :]
    rows3 = lambda v: v[:, None, :]
    w_in_t = jnp.swapaxes(ssd_w_in, 1, 2)
    norm_mix, norm_mlp = rows3(norm_mix_w), rows3(norm_mlp_w)
    conv_b, fno_b = rows3(ssd_conv_b), rows3(fno_b_out)
    ssd_nw = jnp.broadcast_to(ssd_norm_w[:, :, None], (n_ssd, D_INNER, LANES))
    final_w = final_norm_w[None, :]

    gw = HEADS_PER_GROUP * HEAD_DIM
    new_state = jax.ShapeDtypeStruct((batch, n_ssd, 2, N_GROUPS, gw, D_STATE), F32)
    h0_all = state_ssd.reshape(dec_batch, n_ssd, 2, N_GROUPS, gw, D_STATE)

    streams = [
        dict(first_row=0, tokens_per_cond=batch * seq, seqs_per_cond=batch, seg=seq, seq_len=seq,
             scan_seqs=SCAN_SEQS if batch % SCAN_SEQS == 0 else 1,
             fnet_seqs=FNET_SEQS if batch % FNET_SEQS == 0 else 1),
        dict(first_row=1, tokens_per_cond=dec_seq, seqs_per_cond=1, seg=GRID_W, seq_len=dec_seq, scan_seqs=1,
             fnet_seqs=1),
    ]
    acts = [x_prompt.reshape(batch * seq, D_MODEL), x_sample.reshape(dec_batch * dec_seq, D_MODEL)]
    for i in range(DEPTH):
        j = i // 2
        for s, st in enumerate(streams):
            x = acts[s]
            fr, tpc = st["first_row"], st["tokens_per_cond"]
            if i % 2 == 0:
                z, xc, bm, cm, dt = _ssd_in(x, mods, i, j, fr, tpc, st["seg"], norm_mix, w_in_t,
                                            ssd_conv_w, conv_b, dtb)
                if s == 0:
                    y, new_state = _ssd_scan(xc, bm, cm, dt, alog, dskip, st["seq_len"], st["scan_seqs"], j,
                                             state_out=new_state)
                else:
                    y, _ = _ssd_scan(xc, bm, cm, dt, alog, dskip, st["seq_len"], st["scan_seqs"], j, h0=h0_all)
                x, hb = _ssd_out(y, z, x, mods, i, j, fr, tpc, ssd_nw, ssd_w_out, norm_mlp)
            else:
                x, hb = _fnet(x, mods, i, j, fr, st["seqs_per_cond"], st["seq_len"], st["fnet_seqs"], norm_mix,
                              fno_w_out, fno_b, norm_mlp)
            acts[s] = _mlp(x, hb, mods, i, fr, tpc, mlp_w1, mlp_w2, final_w, i == DEPTH - 1)

    y_prompt = acts[0].reshape(batch, seq, D_MODEL)
    y_sample = acts[1].reshape(dec_batch, dec_seq, D_MODEL)
    new_state_ssd = new_state.reshape(batch, n_ssd, 2, N_HEADS, HEAD_DIM, D_STATE)
    return (y_prompt, y_sample, new_state_ssd)
```

```python
import functools

import numpy as np
import jax
import jax.numpy as jnp
from jax import lax
from jax.experimental import pallas as pl
from jax.experimental.pallas import tpu as pltpu

D_MODEL = 1024
DEPTH = 4
GRID_W = 64
D_INNER = 2048
HEAD_DIM = 64
N_HEADS = 32
N_GROUPS = 4
HEADS_PER_GROUP = N_HEADS // N_GROUPS
HG_SHIFT = HEADS_PER_GROUP.bit_length() - 1
D_STATE = 128
CHUNK = 128
D_BC = N_GROUPS * D_STATE
D_XBC = D_INNER + 2 * D_BC
D_ZX = D_INNER + D_XBC
D_IN_PROJ = D_ZX + 2 * N_HEADS
N_FGROUPS = 4
D_FG = D_MODEL // N_FGROUPS
D_FF = 4 * D_MODEL
EPS = 1e-6
N_COND = 8
LOG2E = 1.4426950408889634

LANES = 128
SSD_IN_TILE = 256
SSD_OUT_TILE = 512
SSD_OUT_ROWS = 512
MLP_TILE = 1024
SCAN_SEQS = 4
FNET_SEQS = 4
MAX_SCAN_CHUNKS = 8
XBC_TILE = 512
EPI_TILE = 256
FF_TILE = 1024
ADA_TILE = 1536
VMEM_LIMIT = 56 * 1024 * 1024

F32 = jnp.float32
BF16 = jnp.bfloat16


def _dot(a, b):
    return jnp.dot(a, b, preferred_element_type=F32)


def _dot_nt(a, b):
    return lax.dot_general(a, b, (((1,), (1,)), ((), ())), preferred_element_type=F32)


def _silu(x):
    h = 0.5 * x
    return h + h * jnp.tanh(h)


def _rms(x):
    return x * lax.rsqrt(jnp.mean(x * x, axis=-1, keepdims=True) + EPS)


def _const_spec(shape):
    nd = len(shape)
    return pl.BlockSpec(shape, lambda *_: (0,) * nd, pipeline_mode=pl.Buffered(1))


def _layer_spec(layer, tail):
    nd = len(tail)
    return pl.BlockSpec((None,) + tuple(tail), lambda *_: (layer,) + (0,) * nd,
                        pipeline_mode=pl.Buffered(1))


def _mod_spec(layer, tiles_per_cond, first_row):
    return pl.BlockSpec((None, None, 6, D_MODEL),
                        lambda t, *_: (layer, first_row + t // tiles_per_cond, 0, 0))


def _params(*semantics):
    return pltpu.CompilerParams(dimension_semantics=semantics, vmem_limit_bytes=VMEM_LIMIT)


def _cast_weight(dst_ref, src_ref, n_cols, tile=512):
    for j in range(0, n_cols, tile):
        dst_ref[:, j:j + tile] = src_ref[:, j:j + tile].astype(BF16)


def _ada_kernel(c_ref, w_ref, b_ref, o_ref):
    c = c_ref[...]
    s = _silu(c).astype(BF16)
    o_ref[...] = _dot(s, w_ref[...].astype(BF16)) + b_ref[...]


def _ada_modulation(cond, ada_w, ada_b):
    n_out = 6 * D_MODEL
    out = pl.pallas_call(
        _ada_kernel,
        grid=(DEPTH, n_out // ADA_TILE),
        in_specs=[
            pl.BlockSpec((N_COND, D_MODEL), lambda i, n: (0, 0)),
            pl.BlockSpec((None, D_MODEL, ADA_TILE), lambda i, n: (i, 0, n)),
            pl.BlockSpec((None, 1, ADA_TILE), lambda i, n: (i, 0, n)),
        ],
        out_specs=pl.BlockSpec((None, N_COND, ADA_TILE), lambda i, n: (i, 0, n)),
        out_shape=jax.ShapeDtypeStruct((DEPTH, N_COND, n_out), F32),
        compiler_params=_params("parallel", "parallel"),
        name="ada_mod",
    )(cond, ada_w, ada_b.reshape(DEPTH, 1, n_out))
    return out.reshape(DEPTH, N_COND, 6, D_MODEL)


def _ssd_in_kernel(x_ref, mod_ref, nw_ref, win_ref, cw_ref, cb_ref, dtb_ref,
                   z_ref, xs_ref, b_ref, c_ref, dt_ref, wbf_ref, wdt_ref, *, seg):
    @pl.when(pl.program_id(0) == 0)
    def _():
        for r0 in range(0, D_ZX, XBC_TILE):
            wbf_ref[r0:r0 + XBC_TILE, :] = win_ref[r0:r0 + XBC_TILE, :].astype(BF16)
        n = lax.broadcasted_iota(jnp.int32, (2 * N_HEADS, 2 * N_HEADS), 0)
        k = lax.broadcasted_iota(jnp.int32, (2 * N_HEADS, 2 * N_HEADS), 1)
        src = (((n >> HG_SHIFT) & 1) * N_HEADS + (n >> (HG_SHIFT + 1)) * HEADS_PER_GROUP
               + (n & (HEADS_PER_GROUP - 1)))
        pick = jnp.where(k == src, 1.0, 0.0).astype(BF16)
        wdt_ref[...] = _dot(pick, win_ref[D_ZX:D_IN_PROJ, :].astype(BF16)).astype(BF16)

    tm = x_ref.shape[0]
    h = _rms(x_ref[...]) * nw_ref[...]
    h = h * (1.0 + mod_ref[1:2, :]) + mod_ref[0:1, :]
    hb = h.astype(BF16)
    for j in range(D_INNER // XBC_TILE):
        cols = slice(j * XBC_TILE, (j + 1) * XBC_TILE)
        z_ref[:, cols] = _dot_nt(hb, wbf_ref[cols, :]).astype(BF16)

    row = lax.broadcasted_iota(jnp.int32, (tm, EPI_TILE), 0) & (seg - 1)
    first = row == 0
    last = row == seg - 1
    for j in range(D_XBC // EPI_TILE):
        cols = slice(j * EPI_TILE, (j + 1) * EPI_TILE)
        acc = _dot_nt(hb, wbf_ref[D_INNER + j * EPI_TILE:D_INNER + (j + 1) * EPI_TILE, :])
        up = jnp.where(first, 0.0, pltpu.roll(acc, 1, axis=0))
        dn = jnp.where(last, 0.0, pltpu.roll(acc, tm - 1, axis=0))
        o = up * cw_ref[0:1, cols] + acc * cw_ref[1:2, cols] + dn * cw_ref[2:3, cols] + cb_ref[:, cols]
        o = _silu(o.astype(BF16))
        lo = j * EPI_TILE
        if lo < D_INNER:
            xs_ref[:, lo:lo + EPI_TILE] = o
        elif lo < D_INNER + D_BC:
            b_ref[:, lo - D_INNER:lo - D_INNER + EPI_TILE] = o
        else:
            c_ref[:, lo - D_INNER - D_BC:lo - D_INNER - D_BC + EPI_TILE] = o

    dtT = _dot_nt(wdt_ref[...], hb)
    for k in range(tm // CHUNK):
        v = dtT[:, k * CHUNK:(k + 1) * CHUNK] + dtb_ref[...]
        dt_ref[k] = jnp.maximum(v, 0.0) + jnp.log1p(jnp.exp(-jnp.abs(v)))


def _ssd_in(x, mods, layer, j, first_row, tokens_per_cond, seg, norm_w, w_in, conv_w, conv_b, dtb):
    t_tokens = x.shape[0]
    tm = SSD_IN_TILE
    tok = lambda n: pl.BlockSpec((tm, n), lambda t: (t, 0))
    return pl.pallas_call(
        functools.partial(_ssd_in_kernel, seg=seg),
        grid=(t_tokens // tm,),
        in_specs=[
            tok(D_MODEL),
            _mod_spec(layer, tokens_per_cond // tm, first_row),
            _layer_spec(layer, (1, D_MODEL)),
            _layer_spec(j, (D_IN_PROJ, D_MODEL)),
            _layer_spec(j, (3, D_XBC)),
            _layer_spec(j, (1, D_XBC)),
            _layer_spec(j, (2 * N_HEADS, LANES)),
        ],
        out_specs=[
            tok(D_INNER), tok(D_INNER), tok(D_BC), tok(D_BC),
            pl.BlockSpec((tm // CHUNK, 2 * N_HEADS, LANES), lambda t: (t, 0, 0)),
        ],
        out_shape=[
            jax.ShapeDtypeStruct((t_tokens, D_INNER), BF16),
            jax.ShapeDtypeStruct((t_tokens, D_INNER), BF16),
            jax.ShapeDtypeStruct((t_tokens, D_BC), BF16),
            jax.ShapeDtypeStruct((t_tokens, D_BC), BF16),
            jax.ShapeDtypeStruct((t_tokens // CHUNK, 2 * N_HEADS, LANES), F32),
        ],
        scratch_shapes=[pltpu.VMEM((D_ZX, D_MODEL), BF16), pltpu.VMEM((2 * N_HEADS, D_MODEL), BF16)],
        compiler_params=_params("arbitrary"),
        name="ssd_in",
    )(x, mods, norm_w, w_in, conv_w, conv_b, dtb)


def _split3(a):
    hi = a.astype(BF16)
    r = a - hi.astype(F32)
    mid = r.astype(BF16)
    lo = (r - mid.astype(F32)).astype(BF16)
    return hi, mid, lo


def _dot3(a, m):
    hi, mid, lo = _split3(a)
    return _dot(hi, m) + _dot(mid, m) + _dot(lo, m)


def _ssd_scan_kernel(*refs, seq_len, seqs, has_h0, emit_state, has_alias):
    xs_ref, b_ref, c_ref, dt_ref, alog_ref, dskip_ref = refs[:6]
    pos = 6
    h0_ref = refs[pos] if has_h0 else None
    pos += int(has_h0) + int(has_alias)
    y_ref = refs[pos]
    hf_ref = refs[pos + 1] if emit_state else None

    nc = seq_len // CHUNK
    hg = HEADS_PER_GROUP
    rows = lax.broadcasted_iota(jnp.int32, (CHUNK, CHUNK), 0)
    lanes = lax.broadcasted_iota(jnp.int32, (CHUNK, CHUNK), 1)
    tril = rows >= lanes
    triu = rows <= lanes
    tril_b = jnp.where(tril, 1.0, 0.0).astype(BF16)
    triu_b = jnp.where(triu, 1.0, 0.0).astype(BF16)
    ones_b = jnp.ones((CHUNK, CHUNK), BF16)
    low_half = lanes < HEAD_DIM
    low_b = jnp.where(low_half, 1.0, 0.0).astype(BF16)
    high_b = jnp.where(low_half, 0.0, 1.0).astype(BF16)
    a2 = -jnp.exp(alog_ref[...]) * LOG2E

    sum_tot_b = [jnp.concatenate([triu_b, ones_b], axis=1), jnp.concatenate([tril_b, ones_b], axis=1)]

    def scan_sequence(s):
        below = rows > lanes
        above = rows < lanes
        y, st, cols, cdp, c16 = {}, {}, {}, {}, {}
        for c in range(nc):
            t0 = s * seq_len + c * CHUNK
            dt16 = dt_ref[s * nc + c]
            da16 = dt16 * a2
            parts = jnp.concatenate(_split3(da16), axis=0)
            dts, das, cs_t, w_t, cd, src_t = [], [], [], [], [], []
            for d in range(2):
                res = _dot(parts, sum_tot_b[d])
                sums = (res[0:2 * hg] + res[2 * hg:4 * hg] + res[4 * hg:6 * hg])[d * hg:(d + 1) * hg]
                dts.append(dt16[d * hg:(d + 1) * hg])
                das.append(da16[d * hg:(d + 1) * hg])
                cs_t.append(sums[:, 0:CHUNK])
                tot = sums[:, CHUNK:2 * CHUNK]
                w_t.append(dts[d] * jnp.exp2(tot - cs_t[d]))
                cd.append(jnp.exp2(tot))
                src_t.append(cs_t[d] - jnp.log(dts[d]) * LOG2E)
            diag_t = jnp.log(dts[0] + dts[1]) * LOG2E
            bc = b_ref[t0:t0 + CHUNK, :]
            c16[c] = c_ref[t0:t0 + CHUNK, :]
            cb = _dot_nt(c16[c], bc)
            b_t = bc.astype(F32).T.astype(BF16)
            w16 = [w.astype(BF16) for w in w_t]
            for q in range(hg // 2):
                sl = slice(q * LANES, (q + 1) * LANES)
                xpb = xs_ref[t0:t0 + CHUNK, sl]
                x2 = jnp.concatenate([xpb * low_b, xpb * high_b], axis=0)
                m_parts, bw_parts, col_parts = [], ([], []), ([], [])
                for r in (2 * q, 2 * q + 1):
                    col_f = jnp.sum(jnp.where(tril, das[0][r:r + 1, :], 0.0), axis=1, keepdims=True)
                    col_b = jnp.sum(jnp.where(triu, das[1][r:r + 1, :], 0.0), axis=1, keepdims=True)
                    arg = jnp.where(below, col_f - src_t[0][r:r + 1, :],
                                    jnp.where(above, col_b - src_t[1][r:r + 1, :], diag_t[r:r + 1, :]))
                    m_parts.append((cb * jnp.exp2(arg)).astype(BF16))
                    for d, col in ((0, col_f), (1, col_b)):
                        bw_parts[d].append(b_t * w16[d][r:r + 1, :])
                        col_parts[d].append(col)
                y[c, q] = _dot(jnp.concatenate(m_parts, axis=1), x2) + xpb.astype(F32) * dskip_ref[:, sl]
                for d in range(2):
                    st[d, c, q] = _dot(jnp.concatenate(bw_parts[d], axis=1), x2)
                    cols[d, c, q] = col_parts[d]
                    cdp[d, c, q] = jnp.where(low_half, cd[d][2 * q:2 * q + 1, :], cd[d][2 * q + 1:2 * q + 2, :])

        for d in range(2):
            order = range(nc) if d == 0 else range(nc - 1, -1, -1)
            for q in range(hg // 2):
                sl = slice(q * LANES, (q + 1) * LANES)
                h = h0_ref[s, d, sl, :].T if has_h0 else None
                for c in order:
                    if h is None:
                        h = st[d, c, q]
                    else:
                        ecs = jnp.exp2(jnp.where(low_half, cols[d, c, q][0], cols[d, c, q][1]))
                        y[c, q] = y[c, q] + _dot(c16[c], h.astype(BF16)) * ecs
                        h = h * cdp[d, c, q] + st[d, c, q]
                if emit_state:
                    hf_ref[s, d, sl, :] = h.T
        for c in range(nc):
            for q in range(hg // 2):
                t0 = s * seq_len + c * CHUNK
                y_ref[t0:t0 + CHUNK, q * LANES:(q + 1) * LANES] = y[c, q].astype(BF16)

    for s in range(seqs):
        scan_sequence(s)


def _ssd_scan(xs, bm, cm, dt, alog, dskip, seq_len, seqs, ssd_index, h0=None, state_out=None):
    t_tokens = xs.shape[0]
    rows = seq_len * seqs
    nb = t_tokens // rows
    gw = HEADS_PER_GROUP * HEAD_DIM
    nc = seq_len // CHUNK
    assert seq_len % CHUNK == 0 and nc <= MAX_SCAN_CHUNKS and t_tokens % rows == 0
    in_specs = [
        pl.BlockSpec((rows, gw), lambda b, g: (b, g)),
        pl.BlockSpec((rows, D_STATE), lambda b, g: (b, g)),
        pl.BlockSpec((rows, D_STATE), lambda b, g: (b, g)),
        pl.BlockSpec((seqs * nc, 2 * HEADS_PER_GROUP, LANES), lambda b, g: (b, g, 0)),
        pl.BlockSpec((None, 2 * HEADS_PER_GROUP, LANES), lambda b, g: (ssd_index, g, 0)),
        pl.BlockSpec((None, 1, gw), lambda b, g: (ssd_index, 0, g)),
    ]
    args = [xs, bm, cm, dt, alog, dskip]
    has_h0 = h0 is not None
    if has_h0:
        in_specs.append(pl.BlockSpec((seqs, None, 2, None, gw, D_STATE),
                                     lambda b, g: (b, ssd_index, 0, g, 0, 0)))
        args.append(h0)
    emit_state = state_out is not None
    has_alias = emit_state and not isinstance(state_out, jax.ShapeDtypeStruct)
    out_specs = [pl.BlockSpec((rows, gw), lambda b, g: (b, g))]
    out_shape = [jax.ShapeDtypeStruct((t_tokens, D_INNER), BF16)]
    aliases = {}
    if emit_state:
        if has_alias:
            in_specs.append(pl.BlockSpec(memory_space=pl.ANY))
            args.append(state_out)
            aliases = {len(args) - 1: 1}
        state_sds = jax.ShapeDtypeStruct(state_out.shape, state_out.dtype)
        out_specs.append(pl.BlockSpec((seqs, None, 2, None, gw, D_STATE),
                                      lambda b, g: (b, ssd_index, 0, g, 0, 0)))
        out_shape.append(state_sds)
    res = pl.pallas_call(
        functools.partial(_ssd_scan_kernel, seq_len=seq_len, seqs=seqs, has_h0=has_h0,
                          emit_state=emit_state, has_alias=has_alias),
        grid=(nb, N_GROUPS),
        in_specs=in_specs,
        out_specs=out_specs,
        out_shape=out_shape,
        input_output_aliases=aliases,
        compiler_params=_params("parallel", "parallel"),
        name="ssd_scan",
    )(*args)
    return res if emit_state else (res[0], None)


def _mlp_input(x, mod_ref, nw_ref):
    h = _rms(x) * nw_ref[...]
    return (h * (1.0 + mod_ref[4:5, :]) + mod_ref[3:4, :]).astype(BF16)


def _ssd_out_kernel(y_ref, z_ref, x_ref, mod_ref, nw_ref, wo_ref, nwm_ref, o_ref, hb_ref, wbf_ref):
    @pl.when(pl.program_id(0) == 0)
    def _():
        nw_cols = jnp.concatenate([nw_ref[...]] * (XBC_TILE // LANES), axis=1)
        for j in range(0, D_MODEL, XBC_TILE):
            wbf_ref[:, j:j + XBC_TILE] = (wo_ref[:, j:j + XBC_TILE] * nw_cols).astype(BF16)

    for r0 in range(0, x_ref.shape[0], SSD_OUT_ROWS):
        rs = slice(r0, r0 + SSD_OUT_ROWS)
        g = y_ref[rs, :] * _silu(z_ref[rs, :])
        gf = g.astype(F32)
        scale = lax.rsqrt(jnp.mean(gf * gf, axis=-1, keepdims=True) + EPS)
        x1 = x_ref[rs, :] + mod_ref[2:3, :] * (_dot(g, wbf_ref[...]) * scale)
        o_ref[rs, :] = x1
        hb_ref[rs, :] = _mlp_input(x1, mod_ref, nwm_ref)


def _ssd_out(y, z, x, mods, layer, j, first_row, tokens_per_cond, norm_w, w_out, norm_mlp):
    t_tokens = x.shape[0]
    tm = SSD_OUT_TILE
    tok = lambda n: pl.BlockSpec((tm, n), lambda t: (t, 0))
    return pl.pallas_call(
        _ssd_out_kernel,
        grid=(t_tokens // tm,),
        in_specs=[tok(D_INNER), tok(D_INNER), tok(D_MODEL),
                  _mod_spec(layer, tokens_per_cond // tm, first_row),
                  _layer_spec(j, (D_INNER, LANES)), _layer_spec(j, (D_INNER, D_MODEL)),
                  _layer_spec(layer, (1, D_MODEL))],
        out_specs=[tok(D_MODEL), tok(D_MODEL)],
        out_shape=[jax.ShapeDtypeStruct((t_tokens, D_MODEL), F32),
                   jax.ShapeDtypeStruct((t_tokens, D_MODEL), BF16)],
        scratch_shapes=[pltpu.VMEM((D_INNER, D_MODEL), BF16)],
        compiler_params=_params("arbitrary"),
        name="ssd_out",
    )(y, z, x, mods, norm_w, w_out, norm_mlp)


def _mlp_kernel(x_ref, hb_ref, mod_ref, w1_ref, w2_ref, fw_ref, o_ref, acc_ref, *, final):
    k = pl.program_id(1)
    last = pl.num_programs(1) - 1

    def block():
        a = jnp.maximum(_dot(hb_ref[...], w1_ref[...].astype(BF16)), 0.0)
        return _dot((a * a).astype(BF16), w2_ref[...].astype(BF16))

    @pl.when(k == 0)
    def _():
        acc_ref[...] = block()

    @pl.when(jnp.logical_and(k > 0, k < last))
    def _():
        acc_ref[...] += block()

    @pl.when(k == last)
    def _():
        x2 = x_ref[...] + mod_ref[5:6, :] * (acc_ref[...] + block())
        if final:
            x2 = _rms(x2) * fw_ref[...]
        o_ref[...] = x2


def _mlp(x, hb, mods, layer, first_row, tokens_per_cond, w1, w2, fw, final):
    t_tokens = x.shape[0]
    tm = MLP_TILE
    assert D_FF // FF_TILE >= 2
    tok = pl.BlockSpec((tm, D_MODEL), lambda t, k: (t, 0))
    return pl.pallas_call(
        functools.partial(_mlp_kernel, final=final),
        grid=(t_tokens // tm, D_FF // FF_TILE),
        in_specs=[tok, tok, _mod_spec(layer, tokens_per_cond // tm, first_row),
                  pl.BlockSpec((None, D_MODEL, FF_TILE), lambda t, k: (layer, 0, k)),
                  pl.BlockSpec((None, FF_TILE, D_MODEL), lambda t, k: (layer, k, 0)),
                  _const_spec((1, D_MODEL))],
        out_specs=tok,
        out_shape=jax.ShapeDtypeStruct((t_tokens, D_MODEL), F32),
        scratch_shapes=[pltpu.VMEM((tm, D_MODEL), F32)],
        compiler_params=_params("parallel", "arbitrary"),
        name="mlp",
    )(x, hb, mods, w1, w2, fw)


@functools.lru_cache(maxsize=None)
def _dft_matrices(n):
    k = np.arange(n)
    ang = 2.0 * np.pi * ((k[:, None] * k[None, :]) % n) / n
    return np.cos(ang) / np.sqrt(n), np.sin(ang) / np.sqrt(n)


def _fnet_kernel(x_ref, mod_ref, nw_ref, chan_ref, seqm_ref, wf_ref, bf_ref, nwm_ref, o_ref, hb_ref,
                 f_ref, wbf_ref, *, seq_len):
    @pl.when(pl.program_id(0) == 0)
    def _():
        _cast_weight(wbf_ref, wf_ref, D_MODEL)

    x = x_ref[...]
    h = _rms(x) * nw_ref[...]
    hb = (h * (1.0 + mod_ref[1:2, :]) + mod_ref[0:1, :]).astype(BF16)
    for g in range(N_FGROUPS):
        cols = slice(g * D_FG, (g + 1) * D_FG)
        p = _dot(hb[:, cols], chan_ref[...]).astype(BF16)
        for r0 in range(0, x_ref.shape[0], seq_len):
            rs = slice(r0, r0 + seq_len)
            stacked = jnp.concatenate([p[rs, :D_FG], p[rs, D_FG:]], axis=0)
            f_ref[rs, cols] = _dot(seqm_ref[...], stacked).astype(BF16)
    x1 = x + mod_ref[2:3, :] * (_dot(f_ref[...], wbf_ref[...]) + bf_ref[...])
    o_ref[...] = x1
    hb_ref[...] = _mlp_input(x1, mod_ref, nwm_ref)


def _fnet(x, mods, layer, j, first_row, seqs_per_cond, seq_len, seqs, norm_w, w_f, b_f, norm_mlp):
    t_tokens = x.shape[0]
    rows = seq_len * seqs
    cc, sc = _dft_matrices(D_FG)
    cl, sl = _dft_matrices(seq_len)
    chan = jnp.asarray(np.concatenate([cc, sc], axis=1), F32).astype(BF16)
    seqm = jnp.asarray(np.concatenate([cl, -sl], axis=1), F32).astype(BF16)
    tok = pl.BlockSpec((rows, D_MODEL), lambda t: (t, 0))
    return pl.pallas_call(
        functools.partial(_fnet_kernel, seq_len=seq_len),
        grid=(t_tokens // rows,),
        in_specs=[tok, _mod_spec(layer, seqs_per_cond // seqs, first_row), _layer_spec(layer, (1, D_MODEL)),
                  _const_spec((D_FG, 2 * D_FG)), _const_spec((seq_len, 2 * seq_len)),
                  _layer_spec(j, (D_MODEL, D_MODEL)), _layer_spec(j, (1, D_MODEL)),
                  _layer_spec(layer, (1, D_MODEL))],
        out_specs=[tok, tok],
        out_shape=[jax.ShapeDtypeStruct((t_tokens, D_MODEL), F32),
                   jax.ShapeDtypeStruct((t_tokens, D_MODEL), BF16)],
        scratch_shapes=[pltpu.VMEM((rows, D_MODEL), BF16), pltpu.VMEM((D_MODEL, D_MODEL), BF16)],
        compiler_params=_params("arbitrary"),
        name="fnet",
    )(x, mods, norm_w, chan, seqm, w_f, b_f, norm_mlp)


def _head_perm():
    return np.array([d * N_HEADS + g * HEADS_PER_GROUP + r
                     for g in range(N_GROUPS) for d in range(2) for r in range(HEADS_PER_GROUP)])


def kernel(x_prompt, x_sample, state_ssd, c, c_ctx, ada_w, ada_b, norm_mix_w, norm_mlp_w, ssd_w_in, ssd_conv_w, ssd_conv_b, ssd_dt_bias, ssd_a_log, ssd_d, ssd_norm_w, ssd_w_out, fno_w_out, fno_b_out, mlp_w1, mlp_w2, final_norm_w):
    batch, seq, _ = x_prompt.shape
    dec_batch, dec_seq, _ = x_sample.shape
    n_ssd = ssd_w_in.shape[0]

    cond = jnp.concatenate([c_ctx[None], c, jnp.zeros((N_COND - 1 - dec_batch, D_MODEL), F32)], axis=0)
    mods = _ada_modulation(cond, ada_w, ada_b)

    perm = _head_perm()
    lane_rep = lambda v: jnp.broadcast_to(v.reshape(n_ssd, 2 * N_HEADS)[:, perm][:, :, None],
                                          (n_ssd, 2 * N_HEADS, LANES))
    dtb = lane_rep(ssd_dt_bias)
    alog = lane_rep(ssd_a_log)
    dskip = jnp.repeat(ssd_d, HEAD_DIM, axis=1)[:, None, :]
    rows3 = lambda v: v[:, None, :]
    w_in_t = jnp.swapaxes(ssd_w_in, 1, 2)
    norm_mix, norm_mlp = rows3(norm_mix_w), rows3(norm_mlp_w)
    conv_b, fno_b = rows3(ssd_conv_b), rows3(fno_b_out)
    ssd_nw = jnp.broadcast_to(ssd_norm_w[:, :, None], (n_ssd, D_INNER, LANES))
    final_w = final_norm_w[None, :]

    gw = HEADS_PER_GROUP * HEAD_DIM
    new_state = jax.ShapeDtypeStruct((batch, n_ssd, 2, N_GROUPS, gw, D_STATE), F32)
    h0_all = state_ssd.reshape(dec_batch, n_ssd, 2, N_GROUPS, gw, D_STATE)

    streams = [
        dict(first_row=0, tokens_per_cond=batch * seq, seqs_per_cond=batch, seg=seq, seq_len=seq,
             scan_seqs=SCAN_SEQS if batch % SCAN_SEQS == 0 else 1,
             fnet_seqs=FNET_SEQS if batch % FNET_SEQS == 0 else 1),
        dict(first_row=1, tokens_per_cond=dec_seq, seqs_per_cond=1, seg=GRID_W, seq_len=dec_seq, scan_seqs=1,
             fnet_seqs=1),
    ]
    acts = [x_prompt.reshape(batch * seq, D_MODEL), x_sample.reshape(dec_batch * dec_seq, D_MODEL)]
    for i in range(DEPTH):
        j = i // 2
        for s, st in enumerate(streams):
            x = acts[s]
            fr, tpc = st["first_row"], st["tokens_per_cond"]
            if i % 2 == 0:
                z, xc, bm, cm, dt = _ssd_in(x, mods, i, j, fr, tpc, st["seg"], norm_mix, w_in_t,
                                            ssd_conv_w, conv_b, dtb)
                if s == 0:
                    y, new_state = _ssd_scan(xc, bm, cm, dt, alog, dskip, st["seq_len"], st["scan_seqs"], j,
                                             state_out=new_state)
                else:
                    y, _ = _ssd_scan(xc, bm, cm, dt, alog, dskip, st["seq_len"], st["scan_seqs"], j, h0=h0_all)
                x, hb = _ssd_out(y, z, x, mods, i, j, fr, tpc, ssd_nw, ssd_w_out, norm_mlp)
            else:
                x, hb = _fnet(x, mods, i, j, fr, st["seqs_per_cond"], st["seq_len"], st["fnet_seqs"], norm_mix,
                              fno_w_out, fno_b, norm_mlp)
            acts[s] = _mlp(x, hb, mods, i, fr, tpc, mlp_w1, mlp_w2, final_w, i == DEPTH - 1)

    y_prompt = acts[0].reshape(batch, seq, D_MODEL)
    y_sample = acts[1].reshape(dec_batch, dec_seq, D_MODEL)
    new_state_ssd = new_state.reshape(batch, n_ssd, 2, N_HEADS, HEAD_DIM, D_STATE)
    return (y_prompt, y_sample, new_state_ssd)
```
